```python
import jax, jax.numpy as jnp
from jax import lax
import numpy as np

D_MODEL = 2048
BATCH = 4
SEQ = 2048
DEPTH = 2

N_A_LAYERS = DEPTH // 2
N_B_LAYERS = DEPTH - N_A_LAYERS
EXPAND = 2
D_INNER = EXPAND * D_MODEL

HG_KEY_DIM = 128
HG_HEADS = D_MODEL // HG_KEY_DIM
HG_KEY_TOTAL = HG_HEADS * HG_KEY_DIM
HG_VAL_DIM = D_INNER // HG_HEADS
HG_CHUNK = 64

MLA_HEADS = 32
Q_LORA = 768
KV_LORA = 512
NOPE_DIM = 128
ROPE_DIM = 64
V_DIM = 128
ATTN_BLOCK = 128
ROPE_THETA = 10000.0
EPS = 1e-6

kernel_name = "yoco_hgrn2_mla_hybrid"


def rmsnorm(x, g):
    xf = x.astype(jnp.float32)
    y = xf * lax.rsqrt(jnp.mean(xf * xf, axis=-1, keepdims=True) + EPS)
    return (y * g.astype(jnp.float32)).astype(x.dtype)


def rope_tables(seq):
    pos = jnp.arange(seq, dtype=jnp.float32)
    inv_freq = ROPE_THETA ** (-jnp.arange(0, ROPE_DIM, 2, dtype=jnp.float32) / ROPE_DIM)
    ang = pos[:, None] * inv_freq[None, :]
    return jnp.cos(ang), jnp.sin(ang)


def apply_rope(x, cos, sin):
    xf = x.astype(jnp.float32)
    x1, x2 = jnp.split(xf, 2, axis=-1)
    out = jnp.concatenate([x1 * cos - x2 * sin, x2 * cos + x1 * sin], axis=-1)
    return out.astype(x.dtype)


def hgrn2_mixer(h, w_in, g_norm, w_out, lb):
    B, S, _ = h.shape
    proj = h @ w_in
    q, f, i, g = jnp.split(proj, [HG_KEY_TOTAL, 2 * HG_KEY_TOTAL, 2 * HG_KEY_TOTAL + D_INNER], axis=-1)
    ff = f.astype(jnp.float32)
    lb = lb.astype(jnp.float32)
    log_f = jnp.log(lb + (1.0 - lb) * jax.nn.sigmoid(ff))
    k = (1.0 - lb) * jax.nn.sigmoid(-ff)
    nc = S // HG_CHUNK

    def to_chunks(t, d):
        return t.astype(jnp.float32).reshape(B, nc, HG_CHUNK, HG_HEADS, d).transpose(1, 0, 3, 2, 4)

    qc, kc, gc = to_chunks(q, HG_KEY_DIM), to_chunks(k, HG_KEY_DIM), to_chunks(log_f, HG_KEY_DIM)
    vc = to_chunks(i, HG_VAL_DIM)
    causal = jnp.tril(jnp.ones((HG_CHUNK, HG_CHUNK), dtype=bool))

    def step(state, inp):
        q_, k_, v_, g_ = inp
        b = jnp.cumsum(g_, axis=2)
        o_inter = jnp.einsum('bhck,bhkv->bhcv', q_ * jnp.exp(b), state)
        rel = b[:, :, :, None, :] - b[:, :, None, :, :]
        decay = jnp.exp(jnp.where(causal[None, None, :, :, None], rel, -jnp.inf))
        scores = jnp.einsum('bhtk,bhsk,bhtsk->bhts', q_, k_, decay)
        o_intra = jnp.einsum('bhts,bhsv->bhtv', scores, v_)
        b_last = b[:, :, -1:, :]
        new_state = jnp.exp(b_last[:, :, 0, :])[..., None] * state + \
            jnp.einsum('bhsk,bhsv->bhkv', k_ * jnp.exp(b_last - b), v_)
        return new_state, o_inter + o_intra

    state0 = jnp.zeros((B, HG_HEADS, HG_KEY_DIM, HG_VAL_DIM), jnp.float32)
    _, o = lax.scan(step, state0, (qc, kc, vc, gc))
    o = o.transpose(1, 0, 3, 2, 4).reshape(B, S, HG_HEADS, HG_VAL_DIM).astype(h.dtype)
    o = rmsnorm(o, g_norm).reshape(B, S, D_INNER)
    return (o * jax.nn.silu(g)) @ w_out


def shared_mla_kv(stream, kv_in_norm_g, w_kv_down, kv_norm_g, w_kv_up, cos, sin):
    B, S, _ = stream.shape
    ckv = rmsnorm(stream, kv_in_norm_g) @ w_kv_down
    c, k_rope = jnp.split(ckv, [KV_LORA], axis=-1)
    kv = (rmsnorm(c, kv_norm_g) @ w_kv_up).reshape(B, S, MLA_HEADS, NOPE_DIM + V_DIM)
    k_nope, v = jnp.split(kv, [NOPE_DIM], axis=-1)
    k_rope = apply_rope(k_rope, cos, sin)
    return k_nope, k_rope, v


def causal_mla_attention(q_nope, q_rope, k_nope, k_rope, v):
    B, S, H, _ = q_nope.shape
    nb = S // ATTN_BLOCK
    scale = (NOPE_DIM + ROPE_DIM) ** -0.5
    key_pos = jnp.arange(S)

    def to_blocks(t):
        return t.reshape(B, nb, ATTN_BLOCK, *t.shape[2:]).swapaxes(0, 1)

    def one_block(args):
        qn, qr, start = args
        s = jnp.einsum('bqhd,bkhd->bhqk', qn, k_nope) + jnp.einsum('bqhr,bkr->bhqk', qr, k_rope)
        s = s.astype(jnp.float32) * scale
        q_pos = start + jnp.arange(ATTN_BLOCK)
        s = jnp.where(key_pos[None, :] <= q_pos[:, None], s, -jnp.inf)
        p = jax.nn.softmax(s, axis=-1).astype(v.dtype)
        return jnp.einsum('bhqk,bkhd->bqhd', p, v)

    starts = jnp.arange(nb, dtype=jnp.int32) * ATTN_BLOCK
    out = lax.map(one_block, (to_blocks(q_nope), to_blocks(q_rope), starts))
    return out.swapaxes(0, 1).reshape(B, S, H, V_DIM)


def mla_query_mixer(h, w_in, q_norm_g, w_q_up, w_out, k_nope, k_rope, v, cos, sin):
    B, S, _ = h.shape
    cq, gate = jnp.split(h @ w_in, [Q_LORA], axis=-1)
    q = (rmsnorm(cq, q_norm_g) @ w_q_up).reshape(B, S, MLA_HEADS, NOPE_DIM + ROPE_DIM)
    q_nope, q_rope = jnp.split(q, [NOPE_DIM], axis=-1)
    q_rope = apply_rope(q_rope, cos[:, None, :], sin[:, None, :])
    attn = causal_mla_attention(q_nope, q_rope, k_nope, k_rope, v).reshape(B, S, D_INNER)
    return (attn * jax.nn.silu(gate)) @ w_out


def setup_inputs(seed: int = 0) -> dict:
    key = jax.random.key(seed)
    ks = jax.random.split(key, 20)

    def w(k, shape, fan_in):
        return jax.random.normal(k, shape, jnp.float32) * fan_in ** -0.5

    def gain(k, shape):
        return 1.0 + 0.02 * jax.random.normal(k, shape, jnp.float32)

    hg_in_cols = 2 * HG_KEY_TOTAL + 2 * D_INNER
    return {
        "x": jax.random.normal(ks[0], (BATCH, SEQ, D_MODEL), jnp.float32),
        "norm_g": gain(ks[1], (DEPTH, D_MODEL)),
        "hg_w_in": w(ks[2], (N_A_LAYERS, D_MODEL, hg_in_cols), D_MODEL),
        "hg_g_norm": gain(ks[3], (N_A_LAYERS, HG_VAL_DIM)),
        "hg_w_out": w(ks[4], (N_A_LAYERS, D_INNER, D_MODEL), D_INNER),
        "hg_lb": 0.5 * jax.random.normal(ks[5], (DEPTH, HG_KEY_TOTAL), jnp.float32),
        "kv_in_norm_g": gain(ks[6], (D_MODEL,)),
        "w_kv_down": w(ks[7], (D_MODEL, KV_LORA + ROPE_DIM), D_MODEL),
        "kv_norm_g": gain(ks[8], (KV_LORA,)),
        "w_kv_up": w(ks[9], (KV_LORA, MLA_HEADS * (NOPE_DIM + V_DIM)), KV_LORA),
        "mla_w_in": w(ks[10], (N_B_LAYERS, D_MODEL, Q_LORA + D_INNER), D_MODEL),
        "mla_q_norm_g": gain(ks[11], (N_B_LAYERS, Q_LORA)),
        "mla_w_q_up": w(ks[12], (N_B_LAYERS, Q_LORA, MLA_HEADS * (NOPE_DIM + ROPE_DIM)), Q_LORA),
        "mla_w_out": w(ks[13], (N_B_LAYERS, D_INNER, D_MODEL), D_INNER),
        "final_norm_g": gain(ks[14], (D_MODEL,)),
    }


def reference(x, norm_g, hg_w_in, hg_g_norm, hg_w_out, hg_lb, kv_in_norm_g, w_kv_down, kv_norm_g,
              w_kv_up, mla_w_in, mla_q_norm_g, mla_w_q_up, mla_w_out, final_norm_g):
    S = x.shape[1]
    cos, sin = rope_tables(S)
    lower_bounds = jnp.cumsum(jax.nn.softmax(hg_lb.astype(jnp.float32), axis=0), axis=0)
    h = x
    k_nope = k_rope = v = None
    for layer in range(DEPTH):
        if layer < N_A_LAYERS:
            h = h + hgrn2_mixer(rmsnorm(h, norm_g[layer]), hg_w_in[layer], hg_g_norm[layer],
                                hg_w_out[layer], lower_bounds[layer])
        else:
            j = layer - N_A_LAYERS
            if j == 0:
                k_nope, k_rope, v = shared_mla_kv(h, kv_in_norm_g, w_kv_down, kv_norm_g, w_kv_up, cos, sin)
            h = h + mla_query_mixer(rmsnorm(h, norm_g[layer]), mla_w_in[j], mla_q_norm_g[j],
                                    mla_w_q_up[j], mla_w_out[j], k_nope, k_rope, v, cos, sin)
    return rmsnorm(h, final_norm_g)
```

```python
import functools

import numpy as np
import jax
import jax.numpy as jnp
from jax import lax
from jax.experimental import pallas as pl
from jax.experimental.pallas import tpu as pltpu

F32 = jnp.float32
BF16 = jnp.bfloat16

D_MODEL = 2048
D_INNER = 2 * D_MODEL
HG_KEY_DIM = 128
HG_HEADS = D_MODEL // HG_KEY_DIM
HG_KEY_TOTAL = HG_HEADS * HG_KEY_DIM
HG_VAL_DIM = D_INNER // HG_HEADS
HG_CHUNK = 64
HG_LEVELS = 6
MLA_HEADS = 32
Q_LORA = 768
KV_LORA = 512
NOPE_DIM = 128
ROPE_DIM = 64
V_DIM = 128
ROPE_THETA = 10000.0
EPS = 1e-6

LANES = 128
VMEM_LIMIT = 48 * 1024 * 1024

_NT = (((1,), (1,)), ((), ()))
_TN = (((0,), (0,)), ((), ()))


def _params(n_axes):
    return pltpu.CompilerParams(dimension_semantics=("arbitrary",) * n_axes,
                                vmem_limit_bytes=VMEM_LIMIT)


def _rms(x):
    return x * lax.rsqrt(jnp.mean(x * x, axis=-1, keepdims=True) + EPS)


def _silu(x):
    return x / (1.0 + jnp.exp(-x))


def _nmm_kernel(epilogue, n_extra, x_ref, g_ref, w_ref, *refs):
    extras = refs[:n_extra]
    outs = refs[n_extra:-1]
    xn_ref = refs[-1]

    @pl.when(pl.program_id(1) == 0)
    def _():
        xn_ref[...] = (_rms(x_ref[...]) * g_ref[...]).astype(BF16)

    acc = jnp.dot(xn_ref[...], w_ref[...], preferred_element_type=F32)
    epilogue(acc, extras, outs)


def _nmm_call(x, g, w, *, tm, tn, epilogue, extras=(), extra_specs=(), out_shapes, out_specs, name):
    m, k = x.shape
    n = w.shape[1]
    kernel = functools.partial(_nmm_kernel, epilogue, len(extras))
    return pl.pallas_call(
        kernel,
        grid=(m // tm, n // tn),
        in_specs=[pl.BlockSpec((tm, k), lambda i, j: (i, 0)),
                  pl.BlockSpec((1, k), lambda i, j: (0, 0)),
                  pl.BlockSpec((k, tn), lambda i, j: (0, j))] + list(extra_specs),
        out_specs=out_specs,
        out_shape=out_shapes,
        scratch_shapes=[pltpu.VMEM((tm, k), BF16)],
        compiler_params=_params(2),
        name=name,
    )(x, g, w, *extras)


def _mm_kernel(epilogue, n_extra, x_ref, w_ref, *refs):
    extras = refs[:n_extra]
    outs = refs[n_extra:]
    acc = jnp.dot(x_ref[...], w_ref[...], preferred_element_type=F32)
    epilogue(acc, extras, outs)


def _mm_call(x, w, *, tm, tn, epilogue, extras=(), extra_specs=(), out_shapes, out_specs, name):
    m, k = x.shape
    n = w.shape[1]
    kernel = functools.partial(_mm_kernel, epilogue, len(extras))
    return pl.pallas_call(
        kernel,
        grid=(m // tm, n // tn),
        in_specs=[pl.BlockSpec((tm, k), lambda i, j: (i, 0)),
                  pl.BlockSpec((k, tn), lambda i, j: (0, j))] + list(extra_specs),
        out_specs=out_specs,
        out_shape=out_shapes,
        compiler_params=_params(2),
        name=name,
    )(x, w, *extras)


def _ep_cast(acc, extras, outs):
    outs[0][...] = acc.astype(outs[0].dtype)


def _ep_silu(acc, extras, outs):
    outs[0][...] = _silu(acc).astype(outs[0].dtype)


def _ep_forget_gate(acc, extras, outs):
    lb_logits = extras[0][...]
    mx = jnp.max(lb_logits, axis=0, keepdims=True)
    e = jnp.exp(lb_logits - mx)
    lb = e[0:1, :] / jnp.sum(e, axis=0, keepdims=True)
    t = jnp.exp(-jnp.abs(acc))
    r = 1.0 / (1.0 + t)
    pos = acc >= 0
    sig = jnp.where(pos, r, t * r)
    sig_neg = jnp.where(pos, t * r, r)
    outs[0][...] = jnp.log(lb + (1.0 - lb) * sig)
    outs[1][...] = ((1.0 - lb) * sig_neg).astype(BF16)


def _ep_rms(acc, extras, outs):
    outs[0][...] = (_rms(acc) * extras[0][...]).astype(BF16)


def _ep_kv_down(acc, extras, outs):
    gain, cos4, sin4 = extras
    outs[0][...] = (_rms(acc[:, :KV_LORA]) * gain[...]).astype(BF16)
    kr = acc[:, KV_LORA:KV_LORA + LANES] * cos4[...] + acc[:, KV_LORA + LANES:] * sin4[...]
    outs[1][...] = kr.astype(BF16)


def _ep_q_up(acc, extras, outs):
    cos4, sin4 = extras
    scale = (NOPE_DIM + ROPE_DIM) ** -0.5
    lane = lax.broadcasted_iota(jnp.int32, (acc.shape[0], LANES), 1)
    first_half = lane < ROPE_DIM
    for p in range(acc.shape[1] // 512):
        base = 512 * p
        rope = (acc[:, base + 256:base + 384] * cos4[...] + acc[:, base + 384:base + 512] * sin4[...]) * scale
        outs[0][:, base:base + 128] = (acc[:, base:base + 128] * scale).astype(BF16)
        outs[0][:, base + 128:base + 256] = jnp.where(first_half, rope, 0.0).astype(BF16)
        outs[0][:, base + 256:base + 384] = (acc[:, base + 128:base + 256] * scale).astype(BF16)
        outs[0][:, base + 384:base + 512] = jnp.where(first_half, 0.0, rope).astype(BF16)


def _mm_res_kernel(final_norm, y_ref, w_ref, r_ref, *refs):
    if final_norm:
        g_ref, o_ref, acc_ref = refs
    else:
        o_ref, acc_ref = refs
    k = pl.program_id(1)

    @pl.when(k == 0)
    def _():
        acc_ref[...] = r_ref[...]

    acc_ref[...] += jnp.dot(y_ref[...], w_ref[...], preferred_element_type=F32)

    @pl.when(k == pl.num_programs(1) - 1)
    def _():
        h = acc_ref[...]
        if final_norm:
            h = _rms(h) * g_ref[...]
        o_ref[...] = h


def _mm_res_call(y, w, res, gain, *, tm, tk, name):
    m, kdim = y.shape
    n = w.shape[1]
    final_norm = gain is not None
    in_specs = [pl.BlockSpec((tm, tk), lambda i, k: (i, k)),
                pl.BlockSpec((tk, n), lambda i, k: (k, 0)),
                pl.BlockSpec((tm, n), lambda i, k: (i, 0))]
    args = [y, w, res]
    if final_norm:
        in_specs.append(pl.BlockSpec((1, n), lambda i, k: (0, 0)))
        args.append(gain)
    return pl.pallas_call(
        functools.partial(_mm_res_kernel, final_norm),
        grid=(m // tm, kdim // tk),
        in_specs=in_specs,
        out_specs=pl.BlockSpec((tm, n), lambda i, k: (i, 0)),
        out_shape=jax.ShapeDtypeStruct((m, n), F32),
        scratch_shapes=[pltpu.VMEM((tm, n), F32)],
        compiler_params=_params(2),
        name=name,
    )(*args)


def _hgrn_exponent_matrix():
    c = HG_CHUNK
    t = np.arange(c)[:, None]
    u = np.arange(c)[None, :]
    mats = [(u <= t).astype(np.float32), (u > t).astype(np.float32)]
    for level in range(HG_LEVELS):
        m = 1 << level
        r = ((t >> (level + 1)) << (level + 1)) + m - 1
        upper = ((t >> level) & 1) == 1
        up = ((u > r) & (u <= t)).astype(np.float32)
        lo = ((u > t) & (u <= r)).astype(np.float32)
        mats.append(np.where(upper, up, lo))
    return np.concatenate(mats, axis=0)


def _hgrn_kernel(q_ref, lf_ref, k_ref, v_ref, sg_ref, gn_ref, e_ref, y_ref, st_ref, *, hb, ts):
    c = HG_CHUNK
    dk = HG_KEY_DIM
    dv = HG_VAL_DIM

    @pl.when(pl.program_id(2) == 0)
    def _():
        st_ref[...] = jnp.zeros_like(st_ref)

    e_mat = e_ref[...]
    t_i = lax.broadcasted_iota(jnp.int32, (c, c), 0)
    s_i = lax.broadcasted_iota(jnp.int32, (c, c), 1)
    diff = t_i ^ s_i
    causal = s_i < t_i
    pair_masks = [causal & ((diff >> level) == 1) for level in range(HG_LEVELS)]
    diag = t_i == s_i
    row = lax.broadcasted_iota(jnp.int32, (c, dk), 0)
    uppers = [((row >> level) & 1) == 1 for level in range(HG_LEVELS)]
    gn = gn_ref[...]

    for ci in range(ts // c):
        rows = slice(ci * c, (ci + 1) * c)
        lf = lf_ref[0, rows, :]
        hi = lf.astype(BF16)
        r1 = lf - hi.astype(F32)
        mid = r1.astype(BF16)
        lo = (r1 - mid.astype(F32)).astype(BF16)
        ex = (jnp.dot(e_mat, hi, preferred_element_type=F32)
              + jnp.dot(e_mat, mid, preferred_element_type=F32)
              + jnp.dot(e_mat, lo, preferred_element_type=F32))
        ex = jnp.exp(ex)

        for h in range(hb):
            cs = slice(h * dk, (h + 1) * dk)
            vs = slice(h * dv, (h + 1) * dv)
            q = q_ref[0, rows, cs].astype(F32)
            k = k_ref[0, rows, cs].astype(F32)
            v = v_ref[0, rows, vs]
            st_t = st_ref[h]

            qe = (q * ex[0:c, cs]).astype(BF16)
            o = lax.dot_general(qe, st_t.astype(BF16), _NT, preferred_element_type=F32)

            scores = jnp.where(diag, jnp.sum(q * k, axis=-1, keepdims=True), 0.0)
            for level in range(HG_LEVELS):
                x = ex[(2 + level) * c:(3 + level) * c, cs]
                up = uppers[level]
                w = x * jnp.where(up, q, k)
                q_l = jnp.where(up, w, 0.0).astype(BF16)
                k_l = jnp.where(up, 0.0, w).astype(BF16)
                a_l = lax.dot_general(q_l, k_l, _NT, preferred_element_type=F32)
                scores = scores + jnp.where(pair_masks[level], a_l, 0.0)
            o = o + jnp.dot(scores.astype(BF16), v, preferred_element_type=F32)

            ks = (k * ex[c:2 * c, cs]).astype(BF16)
            decay = ex[c - 1:c, cs]
            st_ref[h] = st_t * decay + lax.dot_general(v, ks, _TN, preferred_element_type=F32)

            y = _rms(o) * gn * sg_ref[0, rows, vs].astype(F32)
            y_ref[0, rows, vs] = y.astype(BF16)


def _hgrn_call(q, lf, k, v, sg, gn, *, hb, ts):
    b, s, _ = q.shape
    e_mat = jnp.asarray(_hgrn_exponent_matrix(), dtype=BF16)
    kernel = functools.partial(_hgrn_kernel, hb=hb, ts=ts)
    key_spec = pl.BlockSpec((1, ts, hb * HG_KEY_DIM), lambda bi, hi, si: (bi, si, hi))
    val_spec = pl.BlockSpec((1, ts, hb * HG_VAL_DIM), lambda bi, hi, si: (bi, si, hi))
    return pl.pallas_call(
        kernel,
        grid=(b, HG_HEADS // hb, s // ts),
        in_specs=[key_spec, key_spec, key_spec, val_spec, val_spec,
                  pl.BlockSpec((1, HG_VAL_DIM), lambda bi, hi, si: (0, 0)),
                  pl.BlockSpec(e_mat.shape, lambda bi, hi, si: (0, 0))],
        out_specs=val_spec,
        out_shape=jax.ShapeDtypeStruct((b, s, D_INNER), BF16),
        scratch_shapes=[pltpu.VMEM((hb, HG_VAL_DIM, HG_KEY_DIM), F32)],
        compiler_params=_params(3),
        name="hgrn2_recurrence",
    )(q, lf, k, v, sg, gn, e_mat)


def _attn_kernel(q_ref, kv_ref, kr_ref, g_ref, o_ref, kcat_ref, *, tq):
    s_len = q_ref.shape[1]
    kcat_ref[:, NOPE_DIM:] = kr_ref[0]
    t_i = lax.broadcasted_iota(jnp.int32, (tq, tq), 0)
    s_i = lax.broadcasted_iota(jnp.int32, (tq, tq), 1)
    causal = s_i <= t_i

    for hh in range(2):
        k_cols = slice(hh * 256, hh * 256 + NOPE_DIM)
        v_cols = slice(hh * 256 + NOPE_DIM, (hh + 1) * 256)
        kcat_ref[:, :NOPE_DIM] = kv_ref[0, :, k_cols]

        for qi in range(s_len // tq):
            rows = slice(qi * tq, (qi + 1) * tq)
            q = q_ref[0, rows, hh * 256:(hh + 1) * 256]

            def block(kk, v, mask, carry):
                m, l, acc = carry
                sc = lax.dot_general(q, kk, _NT, preferred_element_type=F32)
                if mask is not None:
                    sc = jnp.where(mask, sc, -jnp.inf)
                m_new = jnp.maximum(m, jnp.max(sc, axis=-1, keepdims=True))
                p = jnp.exp(sc - m_new)
                alpha = jnp.exp(m - m_new)
                l = alpha * l + jnp.sum(p, axis=-1, keepdims=True)
                acc = alpha * acc + jnp.dot(p.astype(BF16), v, preferred_element_type=F32)
                return m_new, l, acc

            def step(kj, carry):
                off = pl.multiple_of(kj * tq, tq)
                return block(kcat_ref[pl.ds(off, tq), :], kv_ref[0, pl.ds(off, tq), v_cols], None, carry)

            init = (jnp.full((tq, 1), -jnp.inf, F32), jnp.zeros((tq, 1), F32), jnp.zeros((tq, V_DIM), F32))
            carry = lax.fori_loop(0, qi, step, init)
            _, l, acc = block(kcat_ref[rows, :], kv_ref[0, rows, v_cols], causal, carry)
            out = acc / l * g_ref[0, rows, hh * V_DIM:(hh + 1) * V_DIM].astype(F32)
            o_ref[0, rows, hh * V_DIM:(hh + 1) * V_DIM] = out.astype(BF16)


def _attn_call(q, kv, kr, gate, *, tq):
    b, s, _ = q.shape
    kernel = functools.partial(_attn_kernel, tq=tq)
    return pl.pallas_call(
        kernel,
        grid=(b, MLA_HEADS // 2),
        in_specs=[pl.BlockSpec((1, s, 512), lambda bi, hp: (bi, 0, hp)),
                  pl.BlockSpec((1, s, 512), lambda bi, hp: (bi, 0, hp)),
                  pl.BlockSpec((1, s, LANES), lambda bi, hp: (bi, 0, 0)),
                  pl.BlockSpec((1, s, 2 * V_DIM), lambda bi, hp: (bi, 0, hp))],
        out_specs=pl.BlockSpec((1, s, 2 * V_DIM), lambda bi, hp: (bi, 0, hp)),
        out_shape=jax.ShapeDtypeStruct((b, s, D_INNER), BF16),
        scratch_shapes=[pltpu.VMEM((s, 2 * LANES), BF16)],
        compiler_params=_params(2),
        name="mla_flash_attention",
    )(q, kv, kr, gate)


def _rope_tables(seq):
    pos = jnp.arange(seq, dtype=F32)
    inv_freq = ROPE_THETA ** (-jnp.arange(0, ROPE_DIM, 2, dtype=F32) / ROPE_DIM)
    ang = pos[:, None] * inv_freq[None, :]
    cos, sin = jnp.cos(ang), jnp.sin(ang)
    cos4 = jnp.concatenate([cos, cos, cos, cos], axis=-1)
    sin4 = jnp.concatenate([-sin, sin, -sin, sin], axis=-1)
    return cos4, sin4


def _swap_halves(w):
    half = w.shape[-1] // 2
    return jnp.concatenate([w[..., half:], w[..., :half]], axis=-1)


def kernel(x, norm_g, hg_w_in, hg_g_norm, hg_w_out, hg_lb, kv_in_norm_g, w_kv_down, kv_norm_g,
           w_kv_up, mla_w_in, mla_q_norm_g, mla_w_q_up, mla_w_out, final_norm_g):
    b, s, d = x.shape
    m = b * s
    x2 = x.reshape(m, d)
    tm, tn = 512, 1024
    row_spec = lambda n: pl.BlockSpec((tm, n), lambda i, j: (i, j))
    sblocks = s // tm
    rope_spec = pl.BlockSpec((tm, LANES), lambda i, j: (i % sblocks, 0))
    cos4, sin4 = _rope_tables(s)

    g0 = norm_g[0].reshape(1, d)
    w_in = hg_w_in[0]
    c1, c2, c3 = HG_KEY_TOTAL, 2 * HG_KEY_TOTAL, 2 * HG_KEY_TOTAL + D_INNER
    q = _nmm_call(x2, g0, w_in[:, :c1].astype(BF16), tm=tm, tn=tn, epilogue=_ep_cast,
                  out_shapes=[jax.ShapeDtypeStruct((m, HG_KEY_TOTAL), BF16)],
                  out_specs=[row_spec(tn)], name="hg_in_q")[0]
    lf, kg = _nmm_call(x2, g0, w_in[:, c1:c2].astype(BF16), tm=tm, tn=tn, epilogue=_ep_forget_gate,
                       extras=(hg_lb,), extra_specs=(pl.BlockSpec((hg_lb.shape[0], tn), lambda i, j: (0, j)),),
                       out_shapes=[jax.ShapeDtypeStruct((m, HG_KEY_TOTAL), F32),
                                   jax.ShapeDtypeStruct((m, HG_KEY_TOTAL), BF16)],
                       out_specs=[row_spec(tn), row_spec(tn)], name="hg_in_f")
    vi = _nmm_call(x2, g0, w_in[:, c2:c3].astype(BF16), tm=tm, tn=tn, epilogue=_ep_cast,
                   out_shapes=[jax.ShapeDtypeStruct((m, D_INNER), BF16)],
                   out_specs=[row_spec(tn)], name="hg_in_i")[0]
    sg = _nmm_call(x2, g0, w_in[:, c3:].astype(BF16), tm=tm, tn=tn, epilogue=_ep_silu,
                   out_shapes=[jax.ShapeDtypeStruct((m, D_INNER), BF16)],
                   out_specs=[row_spec(tn)], name="hg_in_g")[0]

    y = _hgrn_call(q.reshape(b, s, -1), lf.reshape(b, s, -1), kg.reshape(b, s, -1),
                   vi.reshape(b, s, -1), sg.reshape(b, s, -1), hg_g_norm[0].reshape(1, HG_VAL_DIM),
                   hb=2, ts=512)
    h1 = _mm_res_call(y.reshape(m, D_INNER), hg_w_out[0].astype(BF16), x2, None, tm=512, tk=1024,
                      name="hg_out")

    w_c, w_r = w_kv_down[:, :KV_LORA], w_kv_down[:, KV_LORA:]
    w_r_sw = _swap_halves(w_r)
    w_kvd = jnp.concatenate([w_c, w_r, w_r, w_r_sw, w_r_sw], axis=-1).astype(BF16)
    n_kvd = w_kvd.shape[1]
    cn, kr = _nmm_call(h1, kv_in_norm_g.reshape(1, d), w_kvd, tm=tm, tn=n_kvd, epilogue=_ep_kv_down,
                       extras=(kv_norm_g.reshape(1, KV_LORA), cos4, sin4),
                       extra_specs=(pl.BlockSpec((1, KV_LORA), lambda i, j: (0, 0)), rope_spec, rope_spec),
                       out_shapes=[jax.ShapeDtypeStruct((m, KV_LORA), BF16),
                                   jax.ShapeDtypeStruct((m, LANES), BF16)],
                       out_specs=[pl.BlockSpec((tm, KV_LORA), lambda i, j: (i, 0)),
                                  pl.BlockSpec((tm, LANES), lambda i, j: (i, 0))], name="kv_down")
    kv = _mm_call(cn, w_kv_up.astype(BF16), tm=tm, tn=tn, epilogue=_ep_cast,
                  out_shapes=[jax.ShapeDtypeStruct((m, w_kv_up.shape[1]), BF16)],
                  out_specs=[row_spec(tn)], name="kv_up")[0]

    g1 = norm_g[1].reshape(1, d)
    w_in1 = mla_w_in[0]
    cqn = _nmm_call(h1, g1, w_in1[:, :Q_LORA].astype(BF16), tm=tm, tn=Q_LORA, epilogue=_ep_rms,
                    extras=(mla_q_norm_g[0].reshape(1, Q_LORA),),
                    extra_specs=(pl.BlockSpec((1, Q_LORA), lambda i, j: (0, 0)),),
                    out_shapes=[jax.ShapeDtypeStruct((m, Q_LORA), BF16)],
                    out_specs=[pl.BlockSpec((tm, Q_LORA), lambda i, j: (i, 0))], name="mla_in_q")[0]
    gate = _nmm_call(h1, g1, w_in1[:, Q_LORA:].astype(BF16), tm=tm, tn=tn, epilogue=_ep_silu,
                     out_shapes=[jax.ShapeDtypeStruct((m, D_INNER), BF16)],
                     out_specs=[row_spec(tn)], name="mla_in_gate")[0]

    wq = mla_w_q_up[0].reshape(Q_LORA, MLA_HEADS, NOPE_DIM + ROPE_DIM)
    wq_nope = wq[:, :, :NOPE_DIM].reshape(Q_LORA, MLA_HEADS // 2, 2 * NOPE_DIM)
    wq_rope = wq[:, :, NOPE_DIM:]
    wq_rot = _swap_halves(wq_rope).reshape(Q_LORA, MLA_HEADS // 2, 2 * ROPE_DIM)
    wq_rope = wq_rope.reshape(Q_LORA, MLA_HEADS // 2, 2 * ROPE_DIM)
    wq_ext = jnp.concatenate([wq_nope, wq_rope, wq_rot], axis=-1).reshape(Q_LORA, -1).astype(BF16)
    n_q = wq_ext.shape[1]
    qf = _mm_call(cqn, wq_ext, tm=tm, tn=tn, epilogue=_ep_q_up,
                  extras=(cos4, sin4), extra_specs=(rope_spec, rope_spec),
                  out_shapes=[jax.ShapeDtypeStruct((m, n_q), BF16)],
                  out_specs=[row_spec(tn)], name="mla_q_up")[0]

    attn = _attn_call(qf.reshape(b, s, -1), kv.reshape(b, s, -1), kr.reshape(b, s, -1),
                      gate.reshape(b, s, -1), tq=512)
    out = _mm_res_call(attn.reshape(m, D_INNER), mla_w_out[0].astype(BF16), h1,
                       final_norm_g.reshape(1, d), tm=512, tk=1024, name="mla_out")
    return out.reshape(b, s, d)
```

```python
import functools

import numpy as np
import jax
import jax.numpy as jnp
from jax import lax
from jax.experimental import pallas as pl
from jax.experimental.pallas import tpu as pltpu

F32 = jnp.float32
BF16 = jnp.bfloat16

D_MODEL = 2048
D_INNER = 2 * D_MODEL
HG_KEY_DIM = 128
HG_HEADS = D_MODEL // HG_KEY_DIM
HG_KEY_TOTAL = HG_HEADS * HG_KEY_DIM
HG_VAL_DIM = D_INNER // HG_HEADS
HG_CHUNK = 64
HG_LEVELS = 6
MLA_HEADS = 32
Q_LORA = 768
KV_LORA = 512
NOPE_DIM = 128
ROPE_DIM = 64
V_DIM = 128
ROPE_THETA = 10000.0
EPS = 1e-6

LANES = 128
VMEM_LIMIT = 48 * 1024 * 1024

_NT = (((1,), (1,)), ((), ()))
_TN = (((0,), (0,)), ((), ()))


def _params(n_axes):
    return pltpu.CompilerParams(dimension_semantics=("arbitrary",) * n_axes,
                                vmem_limit_bytes=VMEM_LIMIT)


def _rms(x):
    return x * lax.rsqrt(jnp.mean(x * x, axis=-1, keepdims=True) + EPS)


def _silu(x):
    return x / (1.0 + jnp.exp(-x))


def _nmm_kernel(epilogue, n_extra, x_ref, g_ref, w_ref, *refs):
    extras = refs[:n_extra]
    outs = refs[n_extra:-1]
    xn_ref = refs[-1]

    @pl.when(pl.program_id(1) == 0)
    def _():
        xn_ref[...] = (_rms(x_ref[...]) * g_ref[...]).astype(BF16)

    acc = jnp.dot(xn_ref[...], w_ref[...], preferred_element_type=F32)
    epilogue(acc, extras, outs)


def _nmm_call(x, g, w, *, tm, tn, epilogue, extras=(), extra_specs=(), out_shapes, out_specs, name):
    m, k = x.shape
    n = w.shape[1]
    kernel = functools.partial(_nmm_kernel, epilogue, len(extras))
    return pl.pallas_call(
        kernel,
        grid=(m // tm, n // tn),
        in_specs=[pl.BlockSpec((tm, k), lambda i, j: (i, 0)),
                  pl.BlockSpec((1, k), lambda i, j: (0, 0)),
                  pl.BlockSpec((k, tn), lambda i, j: (0, j))] + list(extra_specs),
        out_specs=out_specs,
        out_shape=out_shapes,
        scratch_shapes=[pltpu.VMEM((tm, k), BF16)],
        compiler_params=_params(2),
        name=name,
    )(x, g, w, *extras)


def _mm_kernel(epilogue, n_extra, x_ref, w_ref, *refs):
    extras = refs[:n_extra]
    outs = refs[n_extra:]
    acc = jnp.dot(x_ref[...], w_ref[...], preferred_element_type=F32)
    epilogue(acc, extras, outs)


def _mm_call(x, w, *, tm, tn, epilogue, extras=(), extra_specs=(), out_shapes, out_specs, name):
    m, k = x.shape
    n = w.shape[1]
    kernel = functools.partial(_mm_kernel, epilogue, len(extras))
    return pl.pallas_call(
        kernel,
        grid=(m // tm, n // tn),
        in_specs=[pl.BlockSpec((tm, k), lambda i, j: (i, 0)),
                  pl.BlockSpec((k, tn), lambda i, j: (0, j))] + list(extra_specs),
        out_specs=out_specs,
        out_shape=out_shapes,
        compiler_params=_params(2),
        name=name,
    )(x, w, *extras)


def _ep_cast(acc, extras, outs):
    outs[0][...] = acc.astype(outs[0].dtype)


def _ep_silu(acc, extras, outs):
    outs[0][...] = _silu(acc).astype(outs[0].dtype)


def _ep_forget_gate(acc, extras, outs):
    lb_logits = extras[0][...]
    mx = jnp.max(lb_logits, axis=0, keepdims=True)
    e = jnp.exp(lb_logits - mx)
    lb = e[0:1, :] / jnp.sum(e, axis=0, keepdims=True)
    t = jnp.exp(-jnp.abs(acc))
    r = 1.0 / (1.0 + t)
    pos = acc >= 0
    sig = jnp.where(pos, r, t * r)
    sig_neg = jnp.where(pos, t * r, r)
    outs[0][...] = jnp.log(lb + (1.0 - lb) * sig)
    outs[1][...] = ((1.0 - lb) * sig_neg).astype(BF16)


def _ep_rms(acc, extras, outs):
    outs[0][...] = (_rms(acc) * extras[0][...]).astype(BF16)


def _ep_kv_down(acc, extras, outs):
    gain, cos4, sin4 = extras
    outs[0][...] = (_rms(acc[:, :KV_LORA]) * gain[...]).astype(BF16)
    kr = acc[:, KV_LORA:KV_LORA + LANES] * cos4[...] + acc[:, KV_LORA + LANES:] * sin4[...]
    outs[1][...] = kr.astype(BF16)


def _ep_q_up(acc, extras, outs):
    cos4, sin4 = extras
    scale = (NOPE_DIM + ROPE_DIM) ** -0.5
    lane = lax.broadcasted_iota(jnp.int32, (acc.shape[0], LANES), 1)
    first_half = lane < ROPE_DIM
    for p in range(acc.shape[1] // 512):
        base = 512 * p
        rope = (acc[:, base + 256:base + 384] * cos4[...] + acc[:, base + 384:base + 512] * sin4[...]) * scale
        outs[0][:, base:base + 128] = (acc[:, base:base + 128] * scale).astype(BF16)
        outs[0][:, base + 128:base + 256] = jnp.where(first_half, rope, 0.0).astype(BF16)
        outs[0][:, base + 256:base + 384] = (acc[:, base + 128:base + 256] * scale).astype(BF16)
        outs[0][:, base + 384:base + 512] = jnp.where(first_half, 0.0, rope).astype(BF16)


def _mm_res_kernel(final_norm, y_ref, w_ref, r_ref, *refs):
    if final_norm:
        g_ref, o_ref, acc_ref = refs
    else:
        o_ref, acc_ref = refs
    k = pl.program_id(1)

    @pl.when(k == 0)
    def _():
        acc_ref[...] = r_ref[...]

    acc_ref[...] += jnp.dot(y_ref[...], w_ref[...], preferred_element_type=F32)

    @pl.when(k == pl.num_programs(1) - 1)
    def _():
        h = acc_ref[...]
        if final_norm:
            h = _rms(h) * g_ref[...]
        o_ref[...] = h


def _mm_res_call(y, w, res, gain, *, tm, tk, name):
    m, kdim = y.shape
    n = w.shape[1]
    final_norm = gain is not None
    in_specs = [pl.BlockSpec((tm, tk), lambda i, k: (i, k)),
                pl.BlockSpec((tk, n), lambda i, k: (k, 0)),
                pl.BlockSpec((tm, n), lambda i, k: (i, 0))]
    args = [y, w, res]
    if final_norm:
        in_specs.append(pl.BlockSpec((1, n), lambda i, k: (0, 0)))
        args.append(gain)
    return pl.pallas_call(
        functools.partial(_mm_res_kernel, final_norm),
        grid=(m // tm, kdim // tk),
        in_specs=in_specs,
        out_specs=pl.BlockSpec((tm, n), lambda i, k: (i, 0)),
        out_shape=jax.ShapeDtypeStruct((m, n), F32),
        scratch_shapes=[pltpu.VMEM((tm, n), F32)],
        compiler_params=_params(2),
        name=name,
    )(*args)


def _hgrn_exponent_matrix():
    c = HG_CHUNK
    t = np.arange(c)[:, None]
    u = np.arange(c)[None, :]
    mats = [(u <= t).astype(np.float32), (u > t).astype(np.float32)]
    for level in range(HG_LEVELS):
        m = 1 << level
        r = ((t >> (level + 1)) << (level + 1)) + m - 1
        upper = ((t >> level) & 1) == 1
        up = ((u > r) & (u <= t)).astype(np.float32)
        lo = ((u > t) & (u <= r)).astype(np.float32)
        mats.append(np.where(upper, up, lo))
    return np.concatenate(mats, axis=0)


def _hgrn_kernel(q_ref, lf_ref, k_ref, v_ref, sg_ref, gn_ref, e_ref, y_ref,
                 st_ref, ex_ref, qe_ref, a_ref, inc_ref, sb_ref, *, hb, ts):
    c = HG_CHUNK
    dk = HG_KEY_DIM
    dv = HG_VAL_DIM

    @pl.when(pl.program_id(2) == 0)
    def _():
        st_ref[...] = jnp.zeros_like(st_ref)

    nc = ts // c
    sub = 8
    e_mat = e_ref[...]
    t_i = lax.broadcasted_iota(jnp.int32, (c, c), 0)
    s_i = lax.broadcasted_iota(jnp.int32, (c, c), 1)
    diff = t_i ^ s_i
    causal = s_i < t_i
    pair_masks = [jnp.where(causal & ((diff >> level) == 1), 1.0, 0.0) for level in range(HG_LEVELS)]
    diag = jnp.where(t_i == s_i, 1.0, 0.0)
    row = lax.broadcasted_iota(jnp.int32, (c, dk), 0)
    upper_f = [jnp.where(((row >> level) & 1) == 1, 1.0, 0.0) for level in range(HG_LEVELS)]
    gn = gn_ref[...]
    zero_rows = jnp.zeros((sub, dk), F32)

    for ci in range(nc):
        lf = lf_ref[0, ci * c:(ci + 1) * c, :]
        hi = lf.astype(BF16)
        lo = (lf - hi.astype(F32)).astype(BF16)
        ex_ref[ci] = jnp.exp(jnp.dot(e_mat, hi, preferred_element_type=F32)
                             + jnp.dot(e_mat, lo, preferred_element_type=F32))

    for ci in range(nc):
        rows = slice(ci * c, (ci + 1) * c)
        for h in range(hb):
            idx = ci * hb + h
            cs = slice(h * dk, (h + 1) * dk)
            q = q_ref[0, rows, cs].astype(F32)
            k = k_ref[0, rows, cs].astype(F32)
            qe_ref[idx] = (q * ex_ref[ci, 0:c, cs]).astype(BF16)
            ks = (k * ex_ref[ci, c:2 * c, cs]).astype(BF16)
            inc_ref[idx] = lax.dot_general(v_ref[0, rows, h * dv:(h + 1) * dv], ks, _TN,
                                           preferred_element_type=F32)

            scores = diag * jnp.sum(q * k, axis=-1, keepdims=True)
            for level in range(HG_LEVELS):
                x = ex_ref[ci, (2 + level) * c:(3 + level) * c, cs]
                if (1 << level) >= sub:
                    qp, kp = [], []
                    for r0 in range(0, c, sub):
                        piece = slice(r0, r0 + sub)
                        if (r0 >> level) & 1:
                            qp.append(x[piece] * q[piece])
                            kp.append(zero_rows)
                        else:
                            qp.append(zero_rows)
                            kp.append(x[piece] * k[piece])
                    q_l = jnp.concatenate(qp, axis=0)
                    k_l = jnp.concatenate(kp, axis=0)
                else:
                    q_l = (x * q) * upper_f[level]
                    k_l = (x * k) * (1.0 - upper_f[level])
                a_l = lax.dot_general(q_l.astype(BF16), k_l.astype(BF16), _NT, preferred_element_type=F32)
                scores = scores + (a_l if level == HG_LEVELS - 1 else a_l * pair_masks[level])
            a_ref[idx] = scores.astype(BF16)

    for h in range(hb):
        st_t = st_ref[h]
        for ci in range(nc):
            idx = ci * hb + h
            sb_ref[idx] = st_t.astype(BF16)
            st_t = st_t * ex_ref[ci, c - 1:c, h * dk:(h + 1) * dk] + inc_ref[idx]
        st_ref[h] = st_t

    for ci in range(nc):
        rows = slice(ci * c, (ci + 1) * c)
        for h in range(hb):
            idx = ci * hb + h
            vs = slice(h * dv, (h + 1) * dv)
            o = (lax.dot_general(qe_ref[idx], sb_ref[idx], _NT, preferred_element_type=F32)
                 + jnp.dot(a_ref[idx], v_ref[0, rows, vs], preferred_element_type=F32))
            y = _rms(o) * gn * sg_ref[0, rows, vs].astype(F32)
            y_ref[0, rows, vs] = y.astype(BF16)


def _hgrn_call(q, lf, k, v, sg, gn, *, hb, ts):
    b, s, _ = q.shape
    e_mat = jnp.asarray(_hgrn_exponent_matrix(), dtype=BF16)
    kernel = functools.partial(_hgrn_kernel, hb=hb, ts=ts)
    nhc = (ts // HG_CHUNK) * hb
    key_spec = pl.BlockSpec((1, ts, hb * HG_KEY_DIM), lambda bi, hi, si: (bi, si, hi))
    val_spec = pl.BlockSpec((1, ts, hb * HG_VAL_DIM), lambda bi, hi, si: (bi, si, hi))
    return pl.pallas_call(
        kernel,
        grid=(b, HG_HEADS // hb, s // ts),
        in_specs=[key_spec, key_spec, key_spec, val_spec, val_spec,
                  pl.BlockSpec((1, HG_VAL_DIM), lambda bi, hi, si: (0, 0)),
                  pl.BlockSpec(e_mat.shape, lambda bi, hi, si: (0, 0))],
        out_specs=val_spec,
        out_shape=jax.ShapeDtypeStruct((b, s, D_INNER), BF16),
        scratch_shapes=[pltpu.VMEM((hb, HG_VAL_DIM, HG_KEY_DIM), F32),
                        pltpu.VMEM((ts // HG_CHUNK,) + (e_mat.shape[0], hb * HG_KEY_DIM), F32),
                        pltpu.VMEM((nhc, HG_CHUNK, HG_KEY_DIM), BF16),
                        pltpu.VMEM((nhc, HG_CHUNK, HG_CHUNK), BF16),
                        pltpu.VMEM((nhc, HG_VAL_DIM, HG_KEY_DIM), F32),
                        pltpu.VMEM((nhc, HG_VAL_DIM, HG_KEY_DIM), BF16)],
        compiler_params=_params(3),
        name="hgrn2_recurrence",
    )(q, lf, k, v, sg, gn, e_mat)


def _attn_kernel(q_ref, kv_ref, kr_ref, g_ref, o_ref, kcat_ref, *, tq):
    s_len = q_ref.shape[1]
    kcat_ref[:, NOPE_DIM:] = kr_ref[0]
    t_i = lax.broadcasted_iota(jnp.int32, (tq, tq), 0)
    s_i = lax.broadcasted_iota(jnp.int32, (tq, tq), 1)
    causal = s_i <= t_i

    for hh in range(2):
        k_cols = slice(hh * 256, hh * 256 + NOPE_DIM)
        v_cols = slice(hh * 256 + NOPE_DIM, (hh + 1) * 256)
        kcat_ref[:, :NOPE_DIM] = kv_ref[0, :, k_cols]

        for qi in range(s_len // tq):
            rows = slice(qi * tq, (qi + 1) * tq)
            q = q_ref[0, rows, hh * 256:(hh + 1) * 256]

            def block(kk, v, mask, carry):
                m, l, acc = carry
                sc = lax.dot_general(q, kk, _NT, preferred_element_type=F32)
                if mask is not None:
                    sc = jnp.where(mask, sc, -jnp.inf)
                m_new = jnp.maximum(m, jnp.max(sc, axis=-1, keepdims=True))
                p = jnp.exp(sc - m_new)
                alpha = jnp.exp(m - m_new)
                l = alpha * l + jnp.sum(p, axis=-1, keepdims=True)
                acc = alpha * acc + jnp.dot(p.astype(BF16), v, preferred_element_type=F32)
                return m_new, l, acc

            def step(kj, carry):
                off = pl.multiple_of(kj * tq, tq)
                return block(kcat_ref[pl.ds(off, tq), :], kv_ref[0, pl.ds(off, tq), v_cols], None, carry)

            init = (jnp.full((tq, 1), -jnp.inf, F32), jnp.zeros((tq, 1), F32), jnp.zeros((tq, V_DIM), F32))
            carry = lax.fori_loop(0, qi, step, init)
            _, l, acc = block(kcat_ref[rows, :], kv_ref[0, rows, v_cols], causal, carry)
            out = acc / l * g_ref[0, rows, hh * V_DIM:(hh + 1) * V_DIM].astype(F32)
            o_ref[0, rows, hh * V_DIM:(hh + 1) * V_DIM] = out.astype(BF16)


def _attn_call(q, kv, kr, gate, *, tq):
    b, s, _ = q.shape
    kernel = functools.partial(_attn_kernel, tq=tq)
    return pl.pallas_call(
        kernel,
        grid=(b, MLA_HEADS // 2),
        in_specs=[pl.BlockSpec((1, s, 512), lambda bi, hp: (bi, 0, hp)),
                  pl.BlockSpec((1, s, 512), lambda bi, hp: (bi, 0, hp)),
                  pl.BlockSpec((1, s, LANES), lambda bi, hp: (bi, 0, 0)),
                  pl.BlockSpec((1, s, 2 * V_DIM), lambda bi, hp: (bi, 0, hp))],
        out_specs=pl.BlockSpec((1, s, 2 * V_DIM), lambda bi, hp: (bi, 0, hp)),
        out_shape=jax.ShapeDtypeStruct((b, s, D_INNER), BF16),
        scratch_shapes=[pltpu.VMEM((s, 2 * LANES), BF16)],
        compiler_params=_params(2),
        name="mla_flash_attention",
    )(q, kv, kr, gate)


def _rope_tables(seq):
    pos = jnp.arange(seq, dtype=F32)
    inv_freq = ROPE_THETA ** (-jnp.arange(0, ROPE_DIM, 2, dtype=F32) / ROPE_DIM)
    ang = pos[:, None] * inv_freq[None, :]
    cos, sin = jnp.cos(ang), jnp.sin(ang)
    cos4 = jnp.concatenate([cos, cos, cos, cos], axis=-1)
    sin4 = jnp.concatenate([-sin, sin, -sin, sin], axis=-1)
    return cos4, sin4


def _swap_halves(w):
    half = w.shape[-1] // 2
    return jnp.concatenate([w[..., half:], w[..., :half]], axis=-1)


def kernel(x, norm_g, hg_w_in, hg_g_norm, hg_w_out, hg_lb, kv_in_norm_g, w_kv_down, kv_norm_g,
           w_kv_up, mla_w_in, mla_q_norm_g, mla_w_q_up, mla_w_out, final_norm_g):
    b, s, d = x.shape
    m = b * s
    x2 = x.reshape(m, d)
    tm, tn = 512, 1024
    row_spec = lambda n: pl.BlockSpec((tm, n), lambda i, j: (i, j))
    sblocks = s // tm
    rope_spec = pl.BlockSpec((tm, LANES), lambda i, j: (i % sblocks, 0))
    cos4, sin4 = _rope_tables(s)

    g0 = norm_g[0].reshape(1, d)
    w_in = hg_w_in[0]
    c1, c2, c3 = HG_KEY_TOTAL, 2 * HG_KEY_TOTAL, 2 * HG_KEY_TOTAL + D_INNER
    q = _nmm_call(x2, g0, w_in[:, :c1].astype(BF16), tm=tm, tn=tn, epilogue=_ep_cast,
                  out_shapes=[jax.ShapeDtypeStruct((m, HG_KEY_TOTAL), BF16)],
                  out_specs=[row_spec(tn)], name="hg_in_q")[0]
    lf, kg = _nmm_call(x2, g0, w_in[:, c1:c2].astype(BF16), tm=tm, tn=tn, epilogue=_ep_forget_gate,
                       extras=(hg_lb,), extra_specs=(pl.BlockSpec((hg_lb.shape[0], tn), lambda i, j: (0, j)),),
                       out_shapes=[jax.ShapeDtypeStruct((m, HG_KEY_TOTAL), F32),
                                   jax.ShapeDtypeStruct((m, HG_KEY_TOTAL), BF16)],
                       out_specs=[row_spec(tn), row_spec(tn)], name="hg_in_f")
    vi = _nmm_call(x2, g0, w_in[:, c2:c3].astype(BF16), tm=tm, tn=tn, epilogue=_ep_cast,
                   out_shapes=[jax.ShapeDtypeStruct((m, D_INNER), BF16)],
                   out_specs=[row_spec(tn)], name="hg_in_i")[0]
    sg = _nmm_call(x2, g0, w_in[:, c3:].astype(BF16), tm=tm, tn=tn, epilogue=_ep_silu,
                   out_shapes=[jax.ShapeDtypeStruct((m, D_INNER), BF16)],
                   out_specs=[row_spec(tn)], name="hg_in_g")[0]

    y = _hgrn_call(q.reshape(b, s, -1), lf.reshape(b, s, -1), kg.reshape(b, s, -1),
                   vi.reshape(b, s, -1), sg.reshape(b, s, -1), hg_g_norm[0].reshape(1, HG_VAL_DIM),
                   hb=2, ts=512)
    h1 = _mm_res_call(y.reshape(m, D_INNER), hg_w_out[0].astype(BF16), x2, None, tm=512, tk=1024,
                      name="hg_out")

    w_c, w_r = w_kv_down[:, :KV_LORA], w_kv_down[:, KV_LORA:]
    w_r_sw = _swap_halves(w_r)
    w_kvd = jnp.concatenate([w_c, w_r, w_r, w_r_sw, w_r_sw], axis=-1).astype(BF16)
    n_kvd = w_kvd.shape[1]
    cn, kr = _nmm_call(h1, kv_in_norm_g.reshape(1, d), w_kvd, tm=tm, tn=n_kvd, epilogue=_ep_kv_down,
                       extras=(kv_norm_g.reshape(1, KV_LORA), cos4, sin4),
                       extra_specs=(pl.BlockSpec((1, KV_LORA), lambda i, j: (0, 0)), rope_spec, rope_spec),
                       out_shapes=[jax.ShapeDtypeStruct((m, KV_LORA), BF16),
                                   jax.ShapeDtypeStruct((m, LANES), BF16)],
                       out_specs=[pl.BlockSpec((tm, KV_LORA), lambda i, j: (i, 0)),
                                  pl.BlockSpec((tm, LANES), lambda i, j: (i, 0))], name="kv_down")
    kv = _mm_call(cn, w_kv_up.astype(BF16), tm=tm, tn=tn, epilogue=_ep_cast,
                  out_shapes=[jax.ShapeDtypeStruct((m, w_kv_up.shape[1]), BF16)],
                  out_specs=[row_spec(tn)], name="kv_up")[0]

    g1 = norm_g[1].reshape(1, d)
    w_in1 = mla_w_in[0]
    cqn = _nmm_call(h1, g1, w_in1[:, :Q_LORA].astype(BF16), tm=tm, tn=Q_LORA, epilogue=_ep_rms,
                    extras=(mla_q_norm_g[0].reshape(1, Q_LORA),),
                    extra_specs=(pl.BlockSpec((1, Q_LORA), lambda i, j: (0, 0)),),
                    out_shapes=[jax.ShapeDtypeStruct((m, Q_LORA), BF16)],
                    out_specs=[pl.BlockSpec((tm, Q_LORA), lambda i, j: (i, 0))], name="mla_in_q")[0]
    gate = _nmm_call(h1, g1, w_in1[:, Q_LORA:].astype(BF16), tm=tm, tn=tn, epilogue=_ep_silu,
                     out_shapes=[jax.ShapeDtypeStruct((m, D_INNER), BF16)],
                     out_specs=[row_spec(tn)], name="mla_in_gate")[0]

    wq = mla_w_q_up[0].reshape(Q_LORA, MLA_HEADS, NOPE_DIM + ROPE_DIM)
    wq_nope = wq[:, :, :NOPE_DIM].reshape(Q_LORA, MLA_HEADS // 2, 2 * NOPE_DIM)
    wq_rope = wq[:, :, NOPE_DIM:]
    wq_rot = _swap_halves(wq_rope).reshape(Q_LORA, MLA_HEADS // 2, 2 * ROPE_DIM)
    wq_rope = wq_rope.reshape(Q_LORA, MLA_HEADS // 2, 2 * ROPE_DIM)
    wq_ext = jnp.concatenate([wq_nope, wq_rope, wq_rot], axis=-1).reshape(Q_LORA, -1).astype(BF16)
    n_q = wq_ext.shape[1]
    qf = _mm_call(cqn, wq_ext, tm=tm, tn=tn, epilogue=_ep_q_up,
                  extras=(cos4, sin4), extra_specs=(rope_spec, rope_spec),
                  out_shapes=[jax.ShapeDtypeStruct((m, n_q), BF16)],
                  out_specs=[row_spec(tn)], name="mla_q_up")[0]

    attn = _attn_call(qf.reshape(b, s, -1), kv.reshape(b, s, -1), kr.reshape(b, s, -1),
                      gate.reshape(b, s, -1), tq=512)
    out = _mm_res_call(attn.reshape(m, D_INNER), mla_w_out[0].astype(BF16), h1,
                       final_norm_g.reshape(1, d), tm=512, tk=1024, name="mla_out")
    return out.reshape(b, s, d)
```

```python
import functools

import numpy as np
import jax
import jax.numpy as jnp
from jax import lax
from jax.experimental import pallas as pl
from jax.experimental.pallas import tpu as pltpu

F32 = jnp.float32
BF16 = jnp.bfloat16

D_MODEL = 2048
D_INNER = 2 * D_MODEL
HG_KEY_DIM = 128
HG_HEADS = D_MODEL // HG_KEY_DIM
HG_KEY_TOTAL = HG_HEADS * HG_KEY_DIM
HG_VAL_DIM = D_INNER // HG_HEADS
HG_CHUNK = 64
HG_LEVELS = 6
MLA_HEADS = 32
Q_LORA = 768
KV_LORA = 512
NOPE_DIM = 128
ROPE_DIM = 64
V_DIM = 128
ROPE_THETA = 10000.0
EPS = 1e-6

LANES = 128
VMEM_LIMIT = 56 * 1024 * 1024

MM_TM, MM_TN, MM_SUB_N = 1024, 2048, 512
RES_TM, RES_TK = 512, 2048

_NT = (((1,), (1,)), ((), ()))
_TN = (((0,), (0,)), ((), ()))


def _params(n_axes):
    return pltpu.CompilerParams(dimension_semantics=("arbitrary",) * n_axes,
                                vmem_limit_bytes=VMEM_LIMIT)


def _rms(x):
    return x * lax.rsqrt(jnp.mean(x * x, axis=-1, keepdims=True) + EPS)


def _silu(x):
    return x / (1.0 + jnp.exp(-x))


def _norm_cast_kernel(x_ref, g_ref, o_ref):
    o_ref[...] = (_rms(x_ref[...]) * g_ref[...]).astype(BF16)


def _norm_cast_call(x, g, *, tm, name):
    m, k = x.shape
    return pl.pallas_call(
        _norm_cast_kernel,
        grid=(m // tm,),
        in_specs=[pl.BlockSpec((tm, k), lambda i: (i, 0)), pl.BlockSpec((1, k), lambda i: (0, 0))],
        out_specs=pl.BlockSpec((tm, k), lambda i: (i, 0)),
        out_shape=jax.ShapeDtypeStruct((m, k), BF16),
        compiler_params=_params(1),
        name=name,
    )(x, g)


def _mm_kernel(epilogue, n_extra, sub_n, x_ref, w_ref, *refs):
    extras = refs[:n_extra]
    outs = refs[n_extra:]
    for c0 in range(0, w_ref.shape[1], sub_n):
        cols = slice(c0, c0 + sub_n)
        acc = jnp.dot(x_ref[...], w_ref[:, cols], preferred_element_type=F32)
        epilogue(acc, cols, extras, outs)


def _mm_call(x, w, *, tm, tn, sub_n, epilogue, extras=(), extra_specs=(), out_shapes, out_specs, name):
    m, k = x.shape
    n = w.shape[1]
    kernel = functools.partial(_mm_kernel, epilogue, len(extras), sub_n)
    return pl.pallas_call(
        kernel,
        grid=(m // tm, n // tn),
        in_specs=[pl.BlockSpec((tm, k), lambda i, j: (i, 0)),
                  pl.BlockSpec((k, tn), lambda i, j: (0, j))] + list(extra_specs),
        out_specs=out_specs,
        out_shape=out_shapes,
        compiler_params=_params(2),
        name=name,
    )(x, w, *extras)


def _ep_cast(acc, cols, extras, outs):
    outs[0][:, cols] = acc.astype(outs[0].dtype)


def _ep_silu(acc, cols, extras, outs):
    outs[0][:, cols] = _silu(acc).astype(outs[0].dtype)


def _ep_forget_gate(acc, cols, extras, outs):
    lb_logits = extras[0][:, cols]
    mx = jnp.max(lb_logits, axis=0, keepdims=True)
    e = jnp.exp(lb_logits - mx)
    lb = e[0:1, :] / jnp.sum(e, axis=0, keepdims=True)
    t = jnp.exp(-jnp.abs(acc))
    r = 1.0 / (1.0 + t)
    pos = acc >= 0
    sig = jnp.where(pos, r, t * r)
    sig_neg = jnp.where(pos, t * r, r)
    outs[0][:, cols] = jnp.log(lb + (1.0 - lb) * sig)
    outs[1][:, cols] = ((1.0 - lb) * sig_neg).astype(BF16)


def _ep_rms(acc, cols, extras, outs):
    outs[0][...] = (_rms(acc) * extras[0][...]).astype(BF16)


def _ep_kv_down(acc, cols, extras, outs):
    gain, cos4, sin4 = extras
    outs[0][...] = (_rms(acc[:, :KV_LORA]) * gain[...]).astype(BF16)
    kr = acc[:, KV_LORA:KV_LORA + LANES] * cos4[...] + acc[:, KV_LORA + LANES:] * sin4[...]
    outs[1][...] = kr.astype(BF16)


def _ep_q_up(acc, cols, extras, outs):
    cos4, sin4 = extras
    scale = (NOPE_DIM + ROPE_DIM) ** -0.5
    lane = lax.broadcasted_iota(jnp.int32, (acc.shape[0], LANES), 1)
    first_half = lane < ROPE_DIM
    base = cols.start
    rope = (acc[:, 256:384] * cos4[...] + acc[:, 384:512] * sin4[...]) * scale
    outs[0][:, base:base + 128] = (acc[:, 0:128] * scale).astype(BF16)
    outs[0][:, base + 128:base + 256] = jnp.where(first_half, rope, 0.0).astype(BF16)
    outs[0][:, base + 256:base + 384] = (acc[:, 128:256] * scale).astype(BF16)
    outs[0][:, base + 384:base + 512] = jnp.where(first_half, 0.0, rope).astype(BF16)


def _mm_res_kernel(final_norm, y_ref, w_ref, r_ref, *refs):
    if final_norm:
        g_ref, o_ref = refs
    else:
        ga_ref, gb_ref, o_ref, na_ref, nb_ref = refs
    k = pl.program_id(1)

    @pl.when(k == 0)
    def _():
        o_ref[...] = r_ref[...]

    o_ref[...] += jnp.dot(y_ref[...], w_ref[...], preferred_element_type=F32)

    @pl.when(k == pl.num_programs(1) - 1)
    def _():
        hn = _rms(o_ref[...])
        if final_norm:
            o_ref[...] = hn * g_ref[...]
        else:
            na_ref[...] = (hn * ga_ref[...]).astype(BF16)
            nb_ref[...] = (hn * gb_ref[...]).astype(BF16)


def _mm_res_call(y, w, res, gains, *, final_norm, tm, tk, name):
    m, kdim = y.shape
    n = w.shape[1]
    row_spec = pl.BlockSpec((tm, n), lambda i, k: (i, 0))
    gain_spec = pl.BlockSpec((1, n), lambda i, k: (0, 0))
    out_shape = [jax.ShapeDtypeStruct((m, n), F32)]
    out_specs = [row_spec]
    if not final_norm:
        out_shape += [jax.ShapeDtypeStruct((m, n), BF16)] * 2
        out_specs += [row_spec, row_spec]
    return pl.pallas_call(
        functools.partial(_mm_res_kernel, final_norm),
        grid=(m // tm, kdim // tk),
        in_specs=[pl.BlockSpec((tm, tk), lambda i, k: (i, k)),
                  pl.BlockSpec((tk, n), lambda i, k: (k, 0)),
                  row_spec] + [gain_spec] * len(gains),
        out_specs=out_specs,
        out_shape=out_shape,
        compiler_params=_params(2),
        name=name,
    )(y, w, res, *gains)


def _hgrn_exponent_matrix():
    c = HG_CHUNK
    t = np.arange(c)[:, None]
    u = np.arange(c)[None, :]
    mats = [(u <= t).astype(np.float32), (u > t).astype(np.float32)]
    for level in range(HG_LEVELS):
        m = 1 << level
        r = ((t >> (level + 1)) << (level + 1)) + m - 1
        upper = ((t >> level) & 1) == 1
        up = ((u > r) & (u <= t)).astype(np.float32)
        lo = ((u > t) & (u <= r)).astype(np.float32)
        mats.append(np.where(upper, up, lo))
    return np.concatenate(mats, axis=0)


def _hgrn_kernel(q_ref, lf_ref, k_ref, v_ref, sg_ref, gn_ref, e_ref, y_ref,
                 st_ref, ex_ref, qe_ref, a_ref, inc_ref, sb_ref, *, hb, ts):
    c = HG_CHUNK
    dk = HG_KEY_DIM
    dv = HG_VAL_DIM

    @pl.when(pl.program_id(2) == 0)
    def _():
        st_ref[...] = jnp.zeros_like(st_ref)

    nc = ts // c
    sub = 8
    e_mat = e_ref[...]
    t_i = lax.broadcasted_iota(jnp.int32, (c, c), 0)
    s_i = lax.broadcasted_iota(jnp.int32, (c, c), 1)
    diff = t_i ^ s_i
    causal = s_i < t_i
    pair_masks = [jnp.where(causal & ((diff >> level) == 1), 1.0, 0.0) for level in range(HG_LEVELS)]
    diag = jnp.where(t_i == s_i, 1.0, 0.0)
    row = lax.broadcasted_iota(jnp.int32, (c, dk), 0)
    upper_f = [jnp.where(((row >> level) & 1) == 1, 1.0, 0.0) for level in range(HG_LEVELS)]
    gn = gn_ref[...]
    zero_rows = jnp.zeros((sub, dk), F32)

    for ci in range(nc):
        lf = lf_ref[0, ci * c:(ci + 1) * c, :]
        hi = lf.astype(BF16)
        lo = (lf - hi.astype(F32)).astype(BF16)
        ex_ref[ci] = jnp.exp(jnp.dot(e_mat, hi, preferred_element_type=F32)
                             + jnp.dot(e_mat, lo, preferred_element_type=F32))

    for ci in range(nc):
        rows = slice(ci * c, (ci + 1) * c)
        for h in range(hb):
            idx = ci * hb + h
            cs = slice(h * dk, (h + 1) * dk)
            q = q_ref[0, rows, cs].astype(F32)
            k = k_ref[0, rows, cs].astype(F32)
            qe_ref[idx] = (q * ex_ref[ci, 0:c, cs]).astype(BF16)
            ks = (k * ex_ref[ci, c:2 * c, cs]).astype(BF16)
            inc_ref[idx] = lax.dot_general(v_ref[0, rows, h * dv:(h + 1) * dv], ks, _TN,
                                           preferred_element_type=F32)

            scores = diag * jnp.sum(q * k, axis=-1, keepdims=True)
            for level in range(HG_LEVELS):
                x = ex_ref[ci, (2 + level) * c:(3 + level) * c, cs]
                if (1 << level) >= sub:
                    qp, kp = [], []
                    for r0 in range(0, c, sub):
                        piece = slice(r0, r0 + sub)
                        if (r0 >> level) & 1:
                            qp.append(x[piece] * q[piece])
                            kp.append(zero_rows)
                        else:
                            qp.append(zero_rows)
                            kp.append(x[piece] * k[piece])
                    q_l = jnp.concatenate(qp, axis=0)
                    k_l = jnp.concatenate(kp, axis=0)
                else:
                    q_l = (x * q) * upper_f[level]
                    k_l = (x * k) * (1.0 - upper_f[level])
                a_l = lax.dot_general(q_l.astype(BF16), k_l.astype(BF16), _NT, preferred_element_type=F32)
                scores = scores + (a_l if level == HG_LEVELS - 1 else a_l * pair_masks[level])
            a_ref[idx] = scores.astype(BF16)

    for h in range(hb):
        st_t = st_ref[h]
        for ci in range(nc):
            idx = ci * hb + h
            sb_ref[idx] = st_t.astype(BF16)
            st_t = st_t * ex_ref[ci, c - 1:c, h * dk:(h + 1) * dk] + inc_ref[idx]
        st_ref[h] = st_t

    for ci in range(nc):
        rows = slice(ci * c, (ci + 1) * c)
        for h in range(hb):
            idx = ci * hb + h
            vs = slice(h * dv, (h + 1) * dv)
            o = (lax.dot_general(qe_ref[idx], sb_ref[idx], _NT, preferred_element_type=F32)
                 + jnp.dot(a_ref[idx], v_ref[0, rows, vs], preferred_element_type=F32))
            y = _rms(o) * gn * sg_ref[0, rows, vs].astype(F32)
            y_ref[0, rows, vs] = y.astype(BF16)


def _hgrn_call(q, lf, k, v, sg, gn, *, hb, ts):
    b, s, _ = q.shape
    e_mat = jnp.asarray(_hgrn_exponent_matrix(), dtype=BF16)
    kernel = functools.partial(_hgrn_kernel, hb=hb, ts=ts)
    nhc = (ts // HG_CHUNK) * hb
    key_spec = pl.BlockSpec((1, ts, hb * HG_KEY_DIM), lambda bi, hi, si: (bi, si, hi))
    val_spec = pl.BlockSpec((1, ts, hb * HG_VAL_DIM), lambda bi, hi, si: (bi, si, hi))
    return pl.pallas_call(
        kernel,
        grid=(b, HG_HEADS // hb, s // ts),
        in_specs=[key_spec, key_spec, key_spec, val_spec, val_spec,
                  pl.BlockSpec((1, HG_VAL_DIM), lambda bi, hi, si: (0, 0)),
                  pl.BlockSpec(e_mat.shape, lambda bi, hi, si: (0, 0))],
        out_specs=val_spec,
        out_shape=jax.ShapeDtypeStruct((b, s, D_INNER), BF16),
        scratch_shapes=[pltpu.VMEM((hb, HG_VAL_DIM, HG_KEY_DIM), F32),
                        pltpu.VMEM((ts // HG_CHUNK,) + (e_mat.shape[0], hb * HG_KEY_DIM), F32),
                        pltpu.VMEM((nhc, HG_CHUNK, HG_KEY_DIM), BF16),
                        pltpu.VMEM((nhc, HG_CHUNK, HG_CHUNK), BF16),
                        pltpu.VMEM((nhc, HG_VAL_DIM, HG_KEY_DIM), F32),
                        pltpu.VMEM((nhc, HG_VAL_DIM, HG_KEY_DIM), BF16)],
        compiler_params=_params(3),
        name="hgrn2_recurrence",
    )(q, lf, k, v, sg, gn, e_mat)


def _attn_kernel(q_ref, kv_ref, kr_ref, g_ref, o_ref, kcat_ref, *, tq):
    s_len = q_ref.shape[1]
    kcat_ref[:, NOPE_DIM:] = kr_ref[0]
    t_i = lax.broadcasted_iota(jnp.int32, (tq, tq), 0)
    s_i = lax.broadcasted_iota(jnp.int32, (tq, tq), 1)
    causal = s_i <= t_i

    for hh in range(2):
        k_cols = slice(hh * 256, hh * 256 + NOPE_DIM)
        v_cols = slice(hh * 256 + NOPE_DIM, (hh + 1) * 256)
        kcat_ref[:, :NOPE_DIM] = kv_ref[0, :, k_cols]

        for qi in range(s_len // tq):
            rows = slice(qi * tq, (qi + 1) * tq)
            q = q_ref[0, rows, hh * 256:(hh + 1) * 256]

            def block(kk, v, mask, carry):
                m, l, acc = carry
                sc = lax.dot_general(q, kk, _NT, preferred_element_type=F32)
                if mask is not None:
                    sc = jnp.where(mask, sc, -jnp.inf)
                m_new = jnp.maximum(m, jnp.max(sc, axis=-1, keepdims=True))
                p = jnp.exp(sc - m_new)
                alpha = jnp.exp(m - m_new)
                l = alpha * l + jnp.sum(p, axis=-1, keepdims=True)
                acc = alpha * acc + jnp.dot(p.astype(BF16), v, preferred_element_type=F32)
                return m_new, l, acc

            def step(kj, carry):
                off = pl.multiple_of(kj * tq, tq)
                return block(kcat_ref[pl.ds(off, tq), :], kv_ref[0, pl.ds(off, tq), v_cols], None, carry)

            init = (jnp.full((tq, 1), -jnp.inf, F32), jnp.zeros((tq, 1), F32), jnp.zeros((tq, V_DIM), F32))
            carry = lax.fori_loop(0, qi, step, init)
            _, l, acc = block(kcat_ref[rows, :], kv_ref[0, rows, v_cols], causal, carry)
            out = acc / l * g_ref[0, rows, hh * V_DIM:(hh + 1) * V_DIM].astype(F32)
            o_ref[0, rows, hh * V_DIM:(hh + 1) * V_DIM] = out.astype(BF16)


def _attn_call(q, kv, kr, gate, *, tq):
    b, s, _ = q.shape
    kernel = functools.partial(_attn_kernel, tq=tq)
    return pl.pallas_call(
        kernel,
        grid=(b, MLA_HEADS // 2),
        in_specs=[pl.BlockSpec((1, s, 512), lambda bi, hp: (bi, 0, hp)),
                  pl.BlockSpec((1, s, 512), lambda bi, hp: (bi, 0, hp)),
                  pl.BlockSpec((1, s, LANES), lambda bi, hp: (bi, 0, 0)),
                  pl.BlockSpec((1, s, 2 * V_DIM), lambda bi, hp: (bi, 0, hp))],
        out_specs=pl.BlockSpec((1, s, 2 * V_DIM), lambda bi, hp: (bi, 0, hp)),
        out_shape=jax.ShapeDtypeStruct((b, s, D_INNER), BF16),
        scratch_shapes=[pltpu.VMEM((s, 2 * LANES), BF16)],
        compiler_params=_params(2),
        name="mla_flash_attention",
    )(q, kv, kr, gate)


def _rope_tables(seq):
    pos = jnp.arange(seq, dtype=F32)
    inv_freq = ROPE_THETA ** (-jnp.arange(0, ROPE_DIM, 2, dtype=F32) / ROPE_DIM)
    ang = pos[:, None] * inv_freq[None, :]
    cos, sin = jnp.cos(ang), jnp.sin(ang)
    cos4 = jnp.concatenate([cos, cos, cos, cos], axis=-1)
    sin4 = jnp.concatenate([-sin, sin, -sin, sin], axis=-1)
    return cos4, sin4


def _swap_halves(w):
    half = w.shape[-1] // 2
    return jnp.concatenate([w[..., half:], w[..., :half]], axis=-1)


def kernel(x, norm_g, hg_w_in, hg_g_norm, hg_w_out, hg_lb, kv_in_norm_g, w_kv_down, kv_norm_g,
           w_kv_up, mla_w_in, mla_q_norm_g, mla_w_q_up, mla_w_out, final_norm_g):
    b, s, d = x.shape
    m = b * s
    x2 = x.reshape(m, d)
    tm, tn, sub_n = MM_TM, MM_TN, MM_SUB_N
    row_spec = lambda n: pl.BlockSpec((tm, n), lambda i, j: (i, j))
    sblocks = s // tm
    rope_spec = pl.BlockSpec((tm, LANES), lambda i, j: (i % sblocks, 0))
    cos4, sin4 = _rope_tables(s)

    def project(xb, w, n_tile, epilogue, out_dtypes, name, extras=(), extra_specs=(), group=sub_n):
        n = w.shape[1]
        outs = _mm_call(xb, w.astype(BF16), tm=tm, tn=n_tile, sub_n=group, epilogue=epilogue,
                        extras=extras, extra_specs=extra_specs,
                        out_shapes=[jax.ShapeDtypeStruct((m, n), dt) for dt in out_dtypes],
                        out_specs=[row_spec(n_tile) for _ in out_dtypes], name=name)
        return outs if len(outs) > 1 else outs[0]

    xn = _norm_cast_call(x2, norm_g[0].reshape(1, d), tm=tm, name="hg_norm")
    w_in = hg_w_in[0]
    c1, c2, c3 = HG_KEY_TOTAL, 2 * HG_KEY_TOTAL, 2 * HG_KEY_TOTAL + D_INNER
    q = project(xn, w_in[:, :c1], tn, _ep_cast, [BF16], "hg_in_q")
    lf, kg = project(xn, w_in[:, c1:c2], tn // 2, _ep_forget_gate, [F32, BF16], "hg_in_f", extras=(hg_lb,),
                     extra_specs=(pl.BlockSpec((hg_lb.shape[0], tn // 2), lambda i, j: (0, j)),))
    vi = project(xn, w_in[:, c2:c3], tn, _ep_cast, [BF16], "hg_in_i")
    sg = project(xn, w_in[:, c3:], tn, _ep_silu, [BF16], "hg_in_g")

    y = _hgrn_call(q.reshape(b, s, -1), lf.reshape(b, s, -1), kg.reshape(b, s, -1),
                   vi.reshape(b, s, -1), sg.reshape(b, s, -1), hg_g_norm[0].reshape(1, HG_VAL_DIM),
                   hb=2, ts=512)
    h1, h1n_kv, h1n_q = _mm_res_call(y.reshape(m, D_INNER), hg_w_out[0].astype(BF16), x2,
                                     (kv_in_norm_g.reshape(1, d), norm_g[1].reshape(1, d)),
                                     final_norm=False, tm=RES_TM, tk=RES_TK, name="hg_out")

    w_c, w_r = w_kv_down[:, :KV_LORA], w_kv_down[:, KV_LORA:]
    w_r_sw = _swap_halves(w_r)
    w_kvd = jnp.concatenate([w_c, w_r, w_r, w_r_sw, w_r_sw], axis=-1)
    n_kvd = w_kvd.shape[1]
    cn, kr = _mm_call(h1n_kv, w_kvd.astype(BF16), tm=tm, tn=n_kvd, sub_n=n_kvd, epilogue=_ep_kv_down,
                      extras=(kv_norm_g.reshape(1, KV_LORA), cos4, sin4),
                      extra_specs=(pl.BlockSpec((1, KV_LORA), lambda i, j: (0, 0)), rope_spec, rope_spec),
                      out_shapes=[jax.ShapeDtypeStruct((m, KV_LORA), BF16),
                                  jax.ShapeDtypeStruct((m, LANES), BF16)],
                      out_specs=[pl.BlockSpec((tm, KV_LORA), lambda i, j: (i, 0)),
                                 pl.BlockSpec((tm, LANES), lambda i, j: (i, 0))], name="kv_down")
    kv = project(cn, w_kv_up, tn, _ep_cast, [BF16], "kv_up")

    w_in1 = mla_w_in[0]
    cqn = _mm_call(h1n_q, w_in1[:, :Q_LORA].astype(BF16), tm=tm, tn=Q_LORA, sub_n=Q_LORA, epilogue=_ep_rms,
                   extras=(mla_q_norm_g[0].reshape(1, Q_LORA),),
                   extra_specs=(pl.BlockSpec((1, Q_LORA), lambda i, j: (0, 0)),),
                   out_shapes=[jax.ShapeDtypeStruct((m, Q_LORA), BF16)],
                   out_specs=[pl.BlockSpec((tm, Q_LORA), lambda i, j: (i, 0))], name="mla_in_q")[0]
    gate = project(h1n_q, w_in1[:, Q_LORA:], tn, _ep_silu, [BF16], "mla_in_gate")

    wq = mla_w_q_up[0].reshape(Q_LORA, MLA_HEADS, NOPE_DIM + ROPE_DIM)
    wq_nope = wq[:, :, :NOPE_DIM].reshape(Q_LORA, MLA_HEADS // 2, 2 * NOPE_DIM)
    wq_rope = wq[:, :, NOPE_DIM:]
    wq_rot = _swap_halves(wq_rope).reshape(Q_LORA, MLA_HEADS // 2, 2 * ROPE_DIM)
    wq_rope = wq_rope.reshape(Q_LORA, MLA_HEADS // 2, 2 * ROPE_DIM)
    wq_ext = jnp.concatenate([wq_nope, wq_rope, wq_rot], axis=-1).reshape(Q_LORA, -1)
    qf = project(cqn, wq_ext, tn, _ep_q_up, [BF16], "mla_q_up", extras=(cos4, sin4),
                 extra_specs=(rope_spec, rope_spec), group=512)

    attn = _attn_call(qf.reshape(b, s, -1), kv.reshape(b, s, -1), kr.reshape(b, s, -1),
                      gate.reshape(b, s, -1), tq=512)
    out = _mm_res_call(attn.reshape(m, D_INNER), mla_w_out[0].astype(BF16), h1,
                       (final_norm_g.reshape(1, d),), final_norm=True, tm=RES_TM, tk=RES_TK,
                       name="mla_out")[0]
    return out.reshape(b, s, d)
```

```python
import functools

import numpy as np
import jax
import jax.numpy as jnp
from jax import lax
from jax.experimental import pallas as pl
from jax.experimental.pallas import tpu as pltpu

F32 = jnp.float32
BF16 = jnp.bfloat16

D_MODEL = 2048
D_INNER = 2 * D_MODEL
HG_KEY_DIM = 128
HG_HEADS = D_MODEL // HG_KEY_DIM
HG_KEY_TOTAL = HG_HEADS * HG_KEY_DIM
HG_VAL_DIM = D_INNER // HG_HEADS
HG_CHUNK = 64
HG_LEVELS = 6
MLA_HEADS = 32
Q_LORA = 768
KV_LORA = 512
NOPE_DIM = 128
ROPE_DIM = 64
V_DIM = 128
ROPE_THETA = 10000.0
EPS = 1e-6
LOG2_E = 1.4426950408889634

LANES = 128
VMEM_LIMIT = 56 * 1024 * 1024

MM_TM, MM_TN, MM_SUB_N = 1024, 2048, 512
RES_TM, RES_TK = 512, 2048

_NT = (((1,), (1,)), ((), ()))
_TN = (((0,), (0,)), ((), ()))


def _params(n_axes):
    return pltpu.CompilerParams(dimension_semantics=("arbitrary",) * n_axes,
                                vmem_limit_bytes=VMEM_LIMIT)


def _rms(x):
    return x * lax.rsqrt(jnp.mean(x * x, axis=-1, keepdims=True) + EPS)


def _silu(x):
    return x / (1.0 + jnp.exp(-x))


def _norm_cast_kernel(x_ref, g_ref, o_ref):
    o_ref[...] = (_rms(x_ref[...]) * g_ref[...]).astype(BF16)


def _norm_cast_call(x, g, *, tm, name):
    m, k = x.shape
    return pl.pallas_call(
        _norm_cast_kernel,
        grid=(m // tm,),
        in_specs=[pl.BlockSpec((tm, k), lambda i: (i, 0)), pl.BlockSpec((1, k), lambda i: (0, 0))],
        out_specs=pl.BlockSpec((tm, k), lambda i: (i, 0)),
        out_shape=jax.ShapeDtypeStruct((m, k), BF16),
        compiler_params=_params(1),
        name=name,
    )(x, g)


def _mm_kernel(epilogue, n_extra, sub_n, x_ref, w_ref, *refs):
    extras = refs[:n_extra]
    outs = refs[n_extra:]
    for c0 in range(0, w_ref.shape[1], sub_n):
        cols = slice(c0, c0 + sub_n)
        acc = jnp.dot(x_ref[...], w_ref[:, cols], preferred_element_type=F32)
        epilogue(acc, cols, extras, outs)


def _mm_call(x, w, *, tm, tn, sub_n, epilogue, extras=(), extra_specs=(), out_shapes, out_specs, name):
    m, k = x.shape
    n = w.shape[1]
    kernel = functools.partial(_mm_kernel, epilogue, len(extras), sub_n)
    return pl.pallas_call(
        kernel,
        grid=(m // tm, n // tn),
        in_specs=[pl.BlockSpec((tm, k), lambda i, j: (i, 0)),
                  pl.BlockSpec((k, tn), lambda i, j: (0, j))] + list(extra_specs),
        out_specs=out_specs,
        out_shape=out_shapes,
        compiler_params=_params(2),
        name=name,
    )(x, w, *extras)


def _ep_cast(acc, cols, extras, outs):
    outs[0][:, cols] = acc.astype(outs[0].dtype)


def _ep_silu(acc, cols, extras, outs):
    outs[0][:, cols] = _silu(acc).astype(outs[0].dtype)


def _ep_forget_gate(acc, cols, extras, outs):
    lb_logits = extras[0][:, cols]
    mx = jnp.max(lb_logits, axis=0, keepdims=True)
    e = jnp.exp(lb_logits - mx)
    lb = e[0:1, :] / jnp.sum(e, axis=0, keepdims=True)
    t = jnp.exp(-jnp.abs(acc))
    r = 1.0 / (1.0 + t)
    pos = acc >= 0
    sig = jnp.where(pos, r, t * r)
    sig_neg = jnp.where(pos, t * r, r)
    outs[0][:, cols] = jnp.log(lb + (1.0 - lb) * sig)
    outs[1][:, cols] = ((1.0 - lb) * sig_neg).astype(BF16)


def _ep_rms(acc, cols, extras, outs):
    outs[0][...] = (_rms(acc) * extras[0][...]).astype(BF16)


def _ep_kv_down(acc, cols, extras, outs):
    gain, cos4, sin4 = extras
    outs[0][...] = (_rms(acc[:, :KV_LORA]) * gain[...]).astype(BF16)
    kr = acc[:, KV_LORA:KV_LORA + LANES] * cos4[...] + acc[:, KV_LORA + LANES:] * sin4[...]
    outs[1][...] = kr.astype(BF16)


def _ep_q_up(acc, cols, extras, outs):
    cos4, sin4 = extras
    scale = (NOPE_DIM + ROPE_DIM) ** -0.5 * LOG2_E
    lane = lax.broadcasted_iota(jnp.int32, (acc.shape[0], LANES), 1)
    first_half = lane < ROPE_DIM
    base = cols.start
    rope = (acc[:, 256:384] * cos4[...] + acc[:, 384:512] * sin4[...]) * scale
    outs[0][:, base:base + 128] = (acc[:, 0:128] * scale).astype(BF16)
    outs[0][:, base + 128:base + 256] = jnp.where(first_half, rope, 0.0).astype(BF16)
    outs[0][:, base + 256:base + 384] = (acc[:, 128:256] * scale).astype(BF16)
    outs[0][:, base + 384:base + 512] = jnp.where(first_half, 0.0, rope).astype(BF16)


def _mm_res_kernel(final_norm, y_ref, w_ref, r_ref, *refs):
    if final_norm:
        g_ref, o_ref = refs
    else:
        ga_ref, gb_ref, o_ref, na_ref, nb_ref = refs
    k = pl.program_id(1)

    @pl.when(k == 0)
    def _():
        o_ref[...] = r_ref[...]

    o_ref[...] += jnp.dot(y_ref[...], w_ref[...], preferred_element_type=F32)

    @pl.when(k == pl.num_programs(1) - 1)
    def _():
        hn = _rms(o_ref[...])
        if final_norm:
            o_ref[...] = hn * g_ref[...]
        else:
            na_ref[...] = (hn * ga_ref[...]).astype(BF16)
            nb_ref[...] = (hn * gb_ref[...]).astype(BF16)


def _mm_res_call(y, w, res, gains, *, final_norm, tm, tk, name):
    m, kdim = y.shape
    n = w.shape[1]
    row_spec = pl.BlockSpec((tm, n), lambda i, k: (i, 0))
    gain_spec = pl.BlockSpec((1, n), lambda i, k: (0, 0))
    out_shape = [jax.ShapeDtypeStruct((m, n), F32)]
    out_specs = [row_spec]
    if not final_norm:
        out_shape += [jax.ShapeDtypeStruct((m, n), BF16)] * 2
        out_specs += [row_spec, row_spec]
    return pl.pallas_call(
        functools.partial(_mm_res_kernel, final_norm),
        grid=(m // tm, kdim // tk),
        in_specs=[pl.BlockSpec((tm, tk), lambda i, k: (i, k)),
                  pl.BlockSpec((tk, n), lambda i, k: (k, 0)),
                  row_spec] + [gain_spec] * len(gains),
        out_specs=out_specs,
        out_shape=out_shape,
        compiler_params=_params(2),
        name=name,
    )(y, w, res, *gains)


def _hgrn_exponent_matrix():
    c = HG_CHUNK
    t = np.arange(c)[:, None]
    u = np.arange(c)[None, :]
    mats = [(u <= t).astype(np.float32), (u > t).astype(np.float32)]
    for level in range(HG_LEVELS):
        m = 1 << level
        r = ((t >> (level + 1)) << (level + 1)) + m - 1
        upper = ((t >> level) & 1) == 1
        up = ((u > r) & (u <= t)).astype(np.float32)
        lo = ((u > t) & (u <= r)).astype(np.float32)
        mats.append(np.where(upper, up, lo))
    return np.concatenate(mats, axis=0)


def _hgrn_kernel(q_ref, lf_ref, k_ref, v_ref, sg_ref, gn_ref, e_ref, y_ref,
                 st_ref, ex_ref, qe_ref, a_ref, inc_ref, sb_ref, *, hb, ts):
    c = HG_CHUNK
    dk = HG_KEY_DIM
    dv = HG_VAL_DIM

    @pl.when(pl.program_id(2) == 0)
    def _():
        st_ref[...] = jnp.zeros_like(st_ref)

    nc = ts // c
    sub = 8
    e_mat = e_ref[...]
    t_i = lax.broadcasted_iota(jnp.int32, (c, c), 0)
    s_i = lax.broadcasted_iota(jnp.int32, (c, c), 1)
    diff = t_i ^ s_i
    causal = s_i < t_i
    pair_masks = [jnp.where(causal & ((diff >> level) == 1), 1.0, 0.0) for level in range(HG_LEVELS)]
    diag = jnp.where(t_i == s_i, 1.0, 0.0)
    row = lax.broadcasted_iota(jnp.int32, (c, dk), 0)
    upper_f = [jnp.where(((row >> level) & 1) == 1, 1.0, 0.0) for level in range(HG_LEVELS)]
    gn = gn_ref[...]
    zero_rows = jnp.zeros((sub, dk), F32)

    for ci in range(nc):
        lf = lf_ref[0, ci * c:(ci + 1) * c, :]
        hi = lf.astype(BF16)
        lo = (lf - hi.astype(F32)).astype(BF16)
        ex_ref[ci] = jnp.exp(jnp.dot(e_mat, hi, preferred_element_type=F32)
                             + jnp.dot(e_mat, lo, preferred_element_type=F32))

    for ci in range(nc):
        rows = slice(ci * c, (ci + 1) * c)
        for h in range(hb):
            idx = ci * hb + h
            cs = slice(h * dk, (h + 1) * dk)
            q = q_ref[0, rows, cs].astype(F32)
            k = k_ref[0, rows, cs].astype(F32)
            qe_ref[idx] = (q * ex_ref[ci, 0:c, cs]).astype(BF16)
            ks = (k * ex_ref[ci, c:2 * c, cs]).astype(BF16)
            inc_ref[idx] = lax.dot_general(v_ref[0, rows, h * dv:(h + 1) * dv], ks, _TN,
                                           preferred_element_type=F32)

            scores = diag * jnp.sum(q * k, axis=-1, keepdims=True)
            for level in range(HG_LEVELS):
                x = ex_ref[ci, (2 + level) * c:(3 + level) * c, cs]
                if (1 << level) >= sub:
                    qp, kp = [], []
                    for r0 in range(0, c, sub):
                        piece = slice(r0, r0 + sub)
                        if (r0 >> level) & 1:
                            qp.append(x[piece] * q[piece])
                            kp.append(zero_rows)
                        else:
                            qp.append(zero_rows)
                            kp.append(x[piece] * k[piece])
                    q_l = jnp.concatenate(qp, axis=0)
                    k_l = jnp.concatenate(kp, axis=0)
                else:
                    q_l = (x * q) * upper_f[level]
                    k_l = (x * k) * (1.0 - upper_f[level])
                a_l = lax.dot_general(q_l.astype(BF16), k_l.astype(BF16), _NT, preferred_element_type=F32)
                scores = scores + (a_l if level == HG_LEVELS - 1 else a_l * pair_masks[level])
            a_ref[idx] = scores.astype(BF16)

    for h in range(hb):
        st_t = st_ref[h]
        for ci in range(nc):
            idx = ci * hb + h
            sb_ref[idx] = st_t.astype(BF16)
            st_t = st_t * ex_ref[ci, c - 1:c, h * dk:(h + 1) * dk] + inc_ref[idx]
        st_ref[h] = st_t

    for ci in range(nc):
        rows = slice(ci * c, (ci + 1) * c)
        for h in range(hb):
            idx = ci * hb + h
            vs = slice(h * dv, (h + 1) * dv)
            o = (lax.dot_general(qe_ref[idx], sb_ref[idx], _NT, preferred_element_type=F32)
                 + jnp.dot(a_ref[idx], v_ref[0, rows, vs], preferred_element_type=F32))
            y = _rms(o) * gn * sg_ref[0, rows, vs].astype(F32)
            y_ref[0, rows, vs] = y.astype(BF16)


def _hgrn_call(q, lf, k, v, sg, gn, *, hb, ts):
    b, s, _ = q.shape
    e_mat = jnp.asarray(_hgrn_exponent_matrix(), dtype=BF16)
    kernel = functools.partial(_hgrn_kernel, hb=hb, ts=ts)
    nhc = (ts // HG_CHUNK) * hb
    key_spec = pl.BlockSpec((1, ts, hb * HG_KEY_DIM), lambda bi, hi, si: (bi, si, hi))
    val_spec = pl.BlockSpec((1, ts, hb * HG_VAL_DIM), lambda bi, hi, si: (bi, si, hi))
    return pl.pallas_call(
        kernel,
        grid=(b, HG_HEADS // hb, s // ts),
        in_specs=[key_spec, key_spec, key_spec, val_spec, val_spec,
                  pl.BlockSpec((1, HG_VAL_DIM), lambda bi, hi, si: (0, 0)),
                  pl.BlockSpec(e_mat.shape, lambda bi, hi, si: (0, 0))],
        out_specs=val_spec,
        out_shape=jax.ShapeDtypeStruct((b, s, D_INNER), BF16),
        scratch_shapes=[pltpu.VMEM((hb, HG_VAL_DIM, HG_KEY_DIM), F32),
                        pltpu.VMEM((ts // HG_CHUNK,) + (e_mat.shape[0], hb * HG_KEY_DIM), F32),
                        pltpu.VMEM((nhc, HG_CHUNK, HG_KEY_DIM), BF16),
                        pltpu.VMEM((nhc, HG_CHUNK, HG_CHUNK), BF16),
                        pltpu.VMEM((nhc, HG_VAL_DIM, HG_KEY_DIM), F32),
                        pltpu.VMEM((nhc, HG_VAL_DIM, HG_KEY_DIM), BF16)],
        compiler_params=_params(3),
        name="hgrn2_recurrence",
    )(q, lf, k, v, sg, gn, e_mat)


def _attn_kernel(q_ref, kv_ref, kr_ref, g_ref, o_ref, kcat_ref, vext_ref, *, tq):
    s_len = q_ref.shape[1]
    t_i = lax.broadcasted_iota(jnp.int32, (tq, tq), 0)
    s_i = lax.broadcasted_iota(jnp.int32, (tq, tq), 1)
    causal = s_i <= t_i

    for hh in range(2):
        kcat_ref[hh, :, :NOPE_DIM] = kv_ref[0, :, hh * 256:hh * 256 + NOPE_DIM]
        kcat_ref[hh, :, NOPE_DIM:] = kr_ref[0]
        vext_ref[hh, :, :V_DIM] = kv_ref[0, :, hh * 256 + NOPE_DIM:(hh + 1) * 256]
        vext_ref[hh, :, V_DIM:] = jnp.ones((s_len, V_DIM), BF16)

        for qi in range(s_len // tq):
            rows = slice(qi * tq, (qi + 1) * tq)
            q = q_ref[0, rows, hh * 256:(hh + 1) * 256]
            m = jnp.full((tq, 1), -jnp.inf, F32)
            acc = jnp.zeros((tq, 2 * V_DIM), F32)
            for kj in range(qi + 1):
                keys = slice(kj * tq, (kj + 1) * tq)
                sc = lax.dot_general(q, kcat_ref[hh, keys, :], _NT, preferred_element_type=F32)
                if kj == qi:
                    sc = jnp.where(causal, sc, -jnp.inf)
                m_new = jnp.maximum(m, jnp.max(sc, axis=-1, keepdims=True))
                p = jnp.exp2(sc - m_new)
                acc = jnp.exp2(m - m_new) * acc + jnp.dot(p.astype(BF16), vext_ref[hh, keys, :],
                                                          preferred_element_type=F32)
                m = m_new
            out = acc[:, :V_DIM] / acc[:, V_DIM:] * g_ref[0, rows, hh * V_DIM:(hh + 1) * V_DIM].astype(F32)
            o_ref[0, rows, hh * V_DIM:(hh + 1) * V_DIM] = out.astype(BF16)


def _attn_call(q, kv, kr, gate, *, tq):
    b, s, _ = q.shape
    kernel = functools.partial(_attn_kernel, tq=tq)
    return pl.pallas_call(
        kernel,
        grid=(b, MLA_HEADS // 2),
        in_specs=[pl.BlockSpec((1, s, 512), lambda bi, hp: (bi, 0, hp)),
                  pl.BlockSpec((1, s, 512), lambda bi, hp: (bi, 0, hp)),
                  pl.BlockSpec((1, s, LANES), lambda bi, hp: (bi, 0, 0)),
                  pl.BlockSpec((1, s, 2 * V_DIM), lambda bi, hp: (bi, 0, hp))],
        out_specs=pl.BlockSpec((1, s, 2 * V_DIM), lambda bi, hp: (bi, 0, hp)),
        out_shape=jax.ShapeDtypeStruct((b, s, D_INNER), BF16),
        scratch_shapes=[pltpu.VMEM((2, s, 2 * LANES), BF16),
                        pltpu.VMEM((2, s, 2 * V_DIM), BF16)],
        compiler_params=_params(2),
        name="mla_flash_attention",
    )(q, kv, kr, gate)


def _rope_tables(seq):
    pos = jnp.arange(seq, dtype=F32)
    inv_freq = ROPE_THETA ** (-jnp.arange(0, ROPE_DIM, 2, dtype=F32) / ROPE_DIM)
    ang = pos[:, None] * inv_freq[None, :]
    cos, sin = jnp.cos(ang), jnp.sin(ang)
    cos4 = jnp.concatenate([cos, cos, cos, cos], axis=-1)
    sin4 = jnp.concatenate([-sin, sin, -sin, sin], axis=-1)
    return cos4, sin4


def _swap_halves(w):
    half = w.shape[-1] // 2
    return jnp.concatenate([w[..., half:], w[..., :half]], axis=-1)


def kernel(x, norm_g, hg_w_in, hg_g_norm, hg_w_out, hg_lb, kv_in_norm_g, w_kv_down, kv_norm_g,
           w_kv_up, mla_w_in, mla_q_norm_g, mla_w_q_up, mla_w_out, final_norm_g):
    b, s, d = x.shape
    m = b * s
    x2 = x.reshape(m, d)
    tm, tn, sub_n = MM_TM, MM_TN, MM_SUB_N
    row_spec = lambda n: pl.BlockSpec((tm, n), lambda i, j: (i, j))
    sblocks = s // tm
    rope_spec = pl.BlockSpec((tm, LANES), lambda i, j: (i % sblocks, 0))
    cos4, sin4 = _rope_tables(s)

    def project(xb, w, n_tile, epilogue, out_dtypes, name, extras=(), extra_specs=(), group=sub_n):
        n = w.shape[1]
        outs = _mm_call(xb, w.astype(BF16), tm=tm, tn=n_tile, sub_n=group, epilogue=epilogue,
                        extras=extras, extra_specs=extra_specs,
                        out_shapes=[jax.ShapeDtypeStruct((m, n), dt) for dt in out_dtypes],
                        out_specs=[row_spec(n_tile) for _ in out_dtypes], name=name)
        return outs if len(outs) > 1 else outs[0]

    xn = _norm_cast_call(x2, norm_g[0].reshape(1, d), tm=tm, name="hg_norm")
    w_in = hg_w_in[0]
    c1, c2, c3 = HG_KEY_TOTAL, 2 * HG_KEY_TOTAL, 2 * HG_KEY_TOTAL + D_INNER
    q = project(xn, w_in[:, :c1], tn, _ep_cast, [BF16], "hg_in_q")
    lf, kg = project(xn, w_in[:, c1:c2], tn // 2, _ep_forget_gate, [F32, BF16], "hg_in_f", extras=(hg_lb,),
                     extra_specs=(pl.BlockSpec((hg_lb.shape[0], tn // 2), lambda i, j: (0, j)),))
    vi = project(xn, w_in[:, c2:c3], tn, _ep_cast, [BF16], "hg_in_i")
    sg = project(xn, w_in[:, c3:], tn, _ep_silu, [BF16], "hg_in_g")

    y = _hgrn_call(q.reshape(b, s, -1), lf.reshape(b, s, -1), kg.reshape(b, s, -1),
                   vi.reshape(b, s, -1), sg.reshape(b, s, -1), hg_g_norm[0].reshape(1, HG_VAL_DIM),
                   hb=2, ts=512)
    h1, h1n_kv, h1n_q = _mm_res_call(y.reshape(m, D_INNER), hg_w_out[0].astype(BF16), x2,
                                     (kv_in_norm_g.reshape(1, d), norm_g[1].reshape(1, d)),
                                     final_norm=False, tm=RES_TM, tk=RES_TK, name="hg_out")

    w_c, w_r = w_kv_down[:, :KV_LORA], w_kv_down[:, KV_LORA:]
    w_r_sw = _swap_halves(w_r)
    w_kvd = jnp.concatenate([w_c, w_r, w_r, w_r_sw, w_r_sw], axis=-1)
    n_kvd = w_kvd.shape[1]
    cn, kr = _mm_call(h1n_kv, w_kvd.astype(BF16), tm=tm, tn=n_kvd, sub_n=n_kvd, epilogue=_ep_kv_down,
                      extras=(kv_norm_g.reshape(1, KV_LORA), cos4, sin4),
                      extra_specs=(pl.BlockSpec((1, KV_LORA), lambda i, j: (0, 0)), rope_spec, rope_spec),
                      out_shapes=[jax.ShapeDtypeStruct((m, KV_LORA), BF16),
                                  jax.ShapeDtypeStruct((m, LANES), BF16)],
                      out_specs=[pl.BlockSpec((tm, KV_LORA), lambda i, j: (i, 0)),
                                 pl.BlockSpec((tm, LANES), lambda i, j: (i, 0))], name="kv_down")
    kv = project(cn, w_kv_up, tn, _ep_cast, [BF16], "kv_up")

    w_in1 = mla_w_in[0]
    cqn = _mm_call(h1n_q, w_in1[:, :Q_LORA].astype(BF16), tm=tm, tn=Q_LORA, sub_n=Q_LORA, epilogue=_ep_rms,
                   extras=(mla_q_norm_g[0].reshape(1, Q_LORA),),
                   extra_specs=(pl.BlockSpec((1, Q_LORA), lambda i, j: (0, 0)),),
                   out_shapes=[jax.ShapeDtypeStruct((m, Q_LORA), BF16)],
                   out_specs=[pl.BlockSpec((tm, Q_LORA), lambda i, j: (i, 0))], name="mla_in_q")[0]
    gate = project(h1n_q, w_in1[:, Q_LORA:], tn, _ep_silu, [BF16], "mla_in_gate")

    wq = mla_w_q_up[0].reshape(Q_LORA, MLA_HEADS, NOPE_DIM + ROPE_DIM)
    wq_nope = wq[:, :, :NOPE_DIM].reshape(Q_LORA, MLA_HEADS // 2, 2 * NOPE_DIM)
    wq_rope = wq[:, :, NOPE_DIM:]
    wq_rot = _swap_halves(wq_rope).reshape(Q_LORA, MLA_HEADS // 2, 2 * ROPE_DIM)
    wq_rope = wq_rope.reshape(Q_LORA, MLA_HEADS // 2, 2 * ROPE_DIM)
    wq_ext = jnp.concatenate([wq_nope, wq_rope, wq_rot], axis=-1).reshape(Q_LORA, -1)
    qf = project(cqn, wq_ext, tn, _ep_q_up, [BF16], "mla_q_up", extras=(cos4, sin4),
                 extra_specs=(rope_spec, rope_spec), group=512)

    attn = _attn_call(qf.reshape(b, s, -1), kv.reshape(b, s, -1), kr.reshape(b, s, -1),
                      gate.reshape(b, s, -1), tq=512)
    out = _mm_res_call(attn.reshape(m, D_INNER), mla_w_out[0].astype(BF16), h1,
                       (final_norm_g.reshape(1, d),), final_norm=True, tm=RES_TM, tk=RES_TK,
                       name="mla_out")[0]
    return out.reshape(b, s, d)
```

```python
import functools

import numpy as np
import jax
import jax.numpy as jnp
from jax import lax
from jax.experimental import pallas as pl
from jax.experimental.pallas import tpu as pltpu

F32 = jnp.float32
BF16 = jnp.bfloat16

D_MODEL = 2048
D_INNER = 2 * D_MODEL
HG_KEY_DIM = 128
HG_HEADS = D_MODEL // HG_KEY_DIM
HG_KEY_TOTAL = HG_HEADS * HG_KEY_DIM
HG_VAL_DIM = D_INNER // HG_HEADS
HG_CHUNK = 64
HG_LEVELS = 6
MLA_HEADS = 32
Q_LORA = 768
KV_LORA = 512
NOPE_DIM = 128
ROPE_DIM = 64
V_DIM = 128
ROPE_THETA = 10000.0
EPS = 1e-6
LOG2_E = 1.4426950408889634

LANES = 128
VMEM_LIMIT = 56 * 1024 * 1024

MM_TM, MM_TN, MM_SUB_N = 1024, 2048, 512
RES_TM, RES_SUB_N = 512, 512

_NT = (((1,), (1,)), ((), ()))
_TN = (((0,), (0,)), ((), ()))


def _params(n_axes):
    return pltpu.CompilerParams(dimension_semantics=("arbitrary",) * n_axes,
                                vmem_limit_bytes=VMEM_LIMIT)


def _rms(x):
    return x * lax.rsqrt(jnp.mean(x * x, axis=-1, keepdims=True) + EPS)


def _silu(x):
    return x / (1.0 + jnp.exp(-x))


def _norm_cast_kernel(x_ref, g_ref, o_ref):
    o_ref[...] = (_rms(x_ref[...]) * g_ref[...]).astype(BF16)


def _norm_cast_call(x, g, *, tm, name):
    m, k = x.shape
    return pl.pallas_call(
        _norm_cast_kernel,
        grid=(m // tm,),
        in_specs=[pl.BlockSpec((tm, k), lambda i: (i, 0)), pl.BlockSpec((1, k), lambda i: (0, 0))],
        out_specs=pl.BlockSpec((tm, k), lambda i: (i, 0)),
        out_shape=jax.ShapeDtypeStruct((m, k), BF16),
        compiler_params=_params(1),
        name=name,
    )(x, g)


def _mm_kernel(epilogue, n_extra, sub_n, x_ref, w_ref, *refs):
    extras = refs[:n_extra]
    outs = refs[n_extra:]
    for c0 in range(0, w_ref.shape[1], sub_n):
        cols = slice(c0, c0 + sub_n)
        acc = jnp.dot(x_ref[...], w_ref[:, cols], preferred_element_type=F32)
        epilogue(acc, cols, extras, outs)


def _mm_call(x, w, *, tm, tn, sub_n, epilogue, extras=(), extra_specs=(), out_shapes, out_specs, name):
    m, k = x.shape
    n = w.shape[1]
    kernel = functools.partial(_mm_kernel, epilogue, len(extras), sub_n)
    return pl.pallas_call(
        kernel,
        grid=(m // tm, n // tn),
        in_specs=[pl.BlockSpec((tm, k), lambda i, j: (i, 0)),
                  pl.BlockSpec((k, tn), lambda i, j: (0, j))] + list(extra_specs),
        out_specs=out_specs,
        out_shape=out_shapes,
        compiler_params=_params(2),
        name=name,
    )(x, w, *extras)


def _ep_cast(acc, cols, extras, outs):
    outs[0][:, cols] = acc.astype(outs[0].dtype)


def _ep_silu(acc, cols, extras, outs):
    outs[0][:, cols] = _silu(acc).astype(outs[0].dtype)


def _ep_forget_gate(acc, cols, extras, outs):
    lb_logits = extras[0][:, cols]
    mx = jnp.max(lb_logits, axis=0, keepdims=True)
    e = jnp.exp(lb_logits - mx)
    lb = e[0:1, :] / jnp.sum(e, axis=0, keepdims=True)
    t = jnp.exp(-jnp.abs(acc))
    r = 1.0 / (1.0 + t)
    pos = acc >= 0
    sig = jnp.where(pos, r, t * r)
    sig_neg = jnp.where(pos, t * r, r)
    outs[0][:, cols] = jnp.log(lb + (1.0 - lb) * sig)
    outs[1][:, cols] = ((1.0 - lb) * sig_neg).astype(BF16)


def _ep_rms(acc, cols, extras, outs):
    outs[0][...] = (_rms(acc) * extras[0][...]).astype(BF16)


def _ep_kv_down(acc, cols, extras, outs):
    gain, cos4, sin4 = extras
    outs[0][...] = (_rms(acc[:, :KV_LORA]) * gain[...]).astype(BF16)
    kr = acc[:, KV_LORA:KV_LORA + LANES] * cos4[...] + acc[:, KV_LORA + LANES:] * sin4[...]
    outs[1][...] = kr.astype(BF16)


def _ep_q_up(acc, cols, extras, outs):
    cos4, sin4 = extras
    scale = (NOPE_DIM + ROPE_DIM) ** -0.5 * LOG2_E
    lane = lax.broadcasted_iota(jnp.int32, (acc.shape[0], LANES), 1)
    first_half = lane < ROPE_DIM
    base = cols.start
    rope = (acc[:, 256:384] * cos4[...] + acc[:, 384:512] * sin4[...]) * scale
    outs[0][:, base:base + 128] = (acc[:, 0:128] * scale).astype(BF16)
    outs[0][:, base + 128:base + 256] = jnp.where(first_half, rope, 0.0).astype(BF16)
    outs[0][:, base + 256:base + 384] = (acc[:, 128:256] * scale).astype(BF16)
    outs[0][:, base + 384:base + 512] = jnp.where(first_half, 0.0, rope).astype(BF16)


def _mm_res_kernel(final_norm, sub_n, y_ref, w_ref, r_ref, *refs):
    if final_norm:
        g_ref, o_ref = refs
    else:
        ga_ref, gb_ref, o_ref, na_ref, nb_ref = refs
    n = w_ref.shape[1]
    ssq = jnp.zeros((y_ref.shape[0], 1), F32)
    for c0 in range(0, n, sub_n):
        cols = slice(c0, c0 + sub_n)
        h = r_ref[:, cols] + jnp.dot(y_ref[...], w_ref[:, cols], preferred_element_type=F32)
        o_ref[:, cols] = h
        ssq = ssq + jnp.sum(h * h, axis=-1, keepdims=True)
    inv = lax.rsqrt(ssq * (1.0 / n) + EPS)
    for c0 in range(0, n, sub_n):
        cols = slice(c0, c0 + sub_n)
        hn = o_ref[:, cols] * inv
        if final_norm:
            o_ref[:, cols] = hn * g_ref[:, cols]
        else:
            na_ref[:, cols] = (hn * ga_ref[:, cols]).astype(BF16)
            nb_ref[:, cols] = (hn * gb_ref[:, cols]).astype(BF16)


def _mm_res_call(y, w, res, gains, *, final_norm, tm, sub_n, name):
    m, kdim = y.shape
    n = w.shape[1]
    row_spec = pl.BlockSpec((tm, n), lambda i: (i, 0))
    gain_spec = pl.BlockSpec((1, n), lambda i: (0, 0))
    out_shape = [jax.ShapeDtypeStruct((m, n), F32)]
    out_specs = [row_spec]
    if not final_norm:
        out_shape += [jax.ShapeDtypeStruct((m, n), BF16)] * 2
        out_specs += [row_spec, row_spec]
    return pl.pallas_call(
        functools.partial(_mm_res_kernel, final_norm, sub_n),
        grid=(m // tm,),
        in_specs=[pl.BlockSpec((tm, kdim), lambda i: (i, 0)),
                  pl.BlockSpec((kdim, n), lambda i: (0, 0), pipeline_mode=pl.Buffered(1)),
                  row_spec] + [gain_spec] * len(gains),
        out_specs=out_specs,
        out_shape=out_shape,
        compiler_params=_params(1),
        name=name,
    )(y, w, res, *gains)


def _hgrn_exponent_matrix():
    c = HG_CHUNK
    t = np.arange(c)[:, None]
    u = np.arange(c)[None, :]
    mats = [(u <= t).astype(np.float32), (u > t).astype(np.float32)]
    for level in range(HG_LEVELS):
        m = 1 << level
        r = ((t >> (level + 1)) << (level + 1)) + m - 1
        upper = ((t >> level) & 1) == 1
        up = ((u > r) & (u <= t)).astype(np.float32)
        lo = ((u > t) & (u <= r)).astype(np.float32)
        mats.append(np.where(upper, up, lo))
    return np.concatenate(mats, axis=0)


def _hgrn_kernel(q_ref, lf_ref, k_ref, v_ref, sg_ref, gn_ref, e_ref, y_ref,
                 st_ref, ex_ref, qe_ref, a_ref, inc_ref, sb_ref, *, hb, ts):
    c = HG_CHUNK
    dk = HG_KEY_DIM
    dv = HG_VAL_DIM

    @pl.when(pl.program_id(2) == 0)
    def _():
        st_ref[...] = jnp.zeros_like(st_ref)

    nc = ts // c
    sub = 8
    e_mat = e_ref[...]
    t_i = lax.broadcasted_iota(jnp.int32, (c, c), 0)
    s_i = lax.broadcasted_iota(jnp.int32, (c, c), 1)
    diff = t_i ^ s_i
    causal = s_i < t_i
    pair_masks = [jnp.where(causal & ((diff >> level) == 1), 1.0, 0.0) for level in range(HG_LEVELS)]
    diag = jnp.where(t_i == s_i, 1.0, 0.0)
    row = lax.broadcasted_iota(jnp.int32, (c, dk), 0)
    uppers = [((row >> level) & 1) == 1 for level in range(HG_LEVELS)]
    gn = gn_ref[...]

    def decays(ci):
        lf = lf_ref[0, ci * c:(ci + 1) * c, :]
        hi = lf.astype(BF16)
        lo = (lf - hi.astype(F32)).astype(BF16)
        ex_ref[ci] = jnp.exp(jnp.dot(e_mat, jnp.concatenate([hi, lo], axis=0), preferred_element_type=F32))

    def chunk_local(ci):
        rows = slice(ci * c, (ci + 1) * c)
        for h in range(hb):
            idx = ci * hb + h
            cs = slice(h * dk, (h + 1) * dk)
            q = q_ref[0, rows, cs].astype(F32)
            k = k_ref[0, rows, cs].astype(F32)
            qe_ref[idx] = (q * ex_ref[ci, 0:c, cs]).astype(BF16)
            ks = (k * ex_ref[ci, c:2 * c, cs]).astype(BF16)
            inc_ref[idx] = lax.dot_general(v_ref[0, rows, h * dv:(h + 1) * dv], ks, _TN,
                                           preferred_element_type=F32)

            scores = diag * jnp.sum(q * k, axis=-1, keepdims=True)
            for level in range(HG_LEVELS):
                x = ex_ref[ci, (2 + level) * c:(3 + level) * c, cs]
                if (1 << level) >= sub:
                    w = jnp.concatenate(
                        [x[r0:r0 + sub] * (q if (r0 >> level) & 1 else k)[r0:r0 + sub] for r0 in range(0, c, sub)],
                        axis=0)
                else:
                    w = x * jnp.where(uppers[level], q, k)
                w = w.astype(BF16)
                scores = scores + lax.dot_general(w, w, _NT, preferred_element_type=F32) * pair_masks[level]
            a_ref[idx] = scores.astype(BF16)

    def recur(ci, states):
        new = []
        for h in range(hb):
            idx = ci * hb + h
            sb_ref[idx] = states[h].astype(BF16)
            new.append(states[h] * ex_ref[ci, c - 1:c, h * dk:(h + 1) * dk] + inc_ref[idx])
        return new

    def outputs(ci):
        rows = slice(ci * c, (ci + 1) * c)
        for h in range(hb):
            idx = ci * hb + h
            vs = slice(h * dv, (h + 1) * dv)
            o = (lax.dot_general(qe_ref[idx], sb_ref[idx], _NT, preferred_element_type=F32)
                 + jnp.dot(a_ref[idx], v_ref[0, rows, vs], preferred_element_type=F32))
            y = _rms(o) * gn * sg_ref[0, rows, vs].astype(F32)
            y_ref[0, rows, vs] = y.astype(BF16)

    states = [st_ref[h] for h in range(hb)]
    for it in range(nc + 2):
        if it < nc:
            decays(it)
        if 0 <= it - 1 < nc:
            chunk_local(it - 1)
            states = recur(it - 1, states)
        if 0 <= it - 2 < nc:
            outputs(it - 2)
    for h in range(hb):
        st_ref[h] = states[h]


def _hgrn_call(q, lf, k, v, sg, gn, *, hb, ts):
    b, s, _ = q.shape
    e_one = _hgrn_exponent_matrix()
    e_mat = jnp.asarray(np.concatenate([e_one, e_one], axis=1), dtype=BF16)
    kernel = functools.partial(_hgrn_kernel, hb=hb, ts=ts)
    nhc = (ts // HG_CHUNK) * hb
    key_spec = pl.BlockSpec((1, ts, hb * HG_KEY_DIM), lambda bi, hi, si: (bi, si, hi))
    val_spec = pl.BlockSpec((1, ts, hb * HG_VAL_DIM), lambda bi, hi, si: (bi, si, hi))
    return pl.pallas_call(
        kernel,
        grid=(b, HG_HEADS // hb, s // ts),
        in_specs=[key_spec, key_spec, key_spec, val_spec, val_spec,
                  pl.BlockSpec((1, HG_VAL_DIM), lambda bi, hi, si: (0, 0)),
                  pl.BlockSpec(e_mat.shape, lambda bi, hi, si: (0, 0))],
        out_specs=val_spec,
        out_shape=jax.ShapeDtypeStruct((b, s, D_INNER), BF16),
        scratch_shapes=[pltpu.VMEM((hb, HG_VAL_DIM, HG_KEY_DIM), F32),
                        pltpu.VMEM((ts // HG_CHUNK,) + (e_mat.shape[0], hb * HG_KEY_DIM), F32),
                        pltpu.VMEM((nhc, HG_CHUNK, HG_KEY_DIM), BF16),
                        pltpu.VMEM((nhc, HG_CHUNK, HG_CHUNK), BF16),
                        pltpu.VMEM((nhc, HG_VAL_DIM, HG_KEY_DIM), F32),
                        pltpu.VMEM((nhc, HG_VAL_DIM, HG_KEY_DIM), BF16)],
        compiler_params=_params(3),
        name="hgrn2_recurrence",
    )(q, lf, k, v, sg, gn, e_mat)


def _attn_kernel(q_ref, kv_ref, kr_ref, g_ref, o_ref, kcat_ref, vext_ref, *, tq):
    s_len = q_ref.shape[1]
    t_i = lax.broadcasted_iota(jnp.int32, (tq, tq), 0)
    s_i = lax.broadcasted_iota(jnp.int32, (tq, tq), 1)
    causal = s_i <= t_i

    for hh in range(2):
        kcat_ref[hh, :, :NOPE_DIM] = kv_ref[0, :, hh * 256:hh * 256 + NOPE_DIM]
        kcat_ref[hh, :, NOPE_DIM:] = kr_ref[0]
        vext_ref[hh, :, :V_DIM] = kv_ref[0, :, hh * 256 + NOPE_DIM:(hh + 1) * 256]
        vext_ref[hh, :, V_DIM:] = jnp.ones((s_len, V_DIM), BF16)

        for qi in range(s_len // tq):
            rows = slice(qi * tq, (qi + 1) * tq)
            q = q_ref[0, rows, hh * 256:(hh + 1) * 256]
            m = jnp.full((tq, 1), -jnp.inf, F32)
            acc = jnp.zeros((tq, 2 * V_DIM), F32)
            for kj in range(qi + 1):
                keys = slice(kj * tq, (kj + 1) * tq)
                sc = lax.dot_general(q, kcat_ref[hh, keys, :], _NT, preferred_element_type=F32)
                if kj == qi:
                    sc = jnp.where(causal, sc, -jnp.inf)
                m_new = jnp.maximum(m, jnp.max(sc, axis=-1, keepdims=True))
                p = jnp.exp2(sc - m_new)
                acc = jnp.exp2(m - m_new) * acc + jnp.dot(p.astype(BF16), vext_ref[hh, keys, :],
                                                          preferred_element_type=F32)
                m = m_new
            out = acc[:, :V_DIM] / acc[:, V_DIM:] * g_ref[0, rows, hh * V_DIM:(hh + 1) * V_DIM].astype(F32)
            o_ref[0, rows, hh * V_DIM:(hh + 1) * V_DIM] = out.astype(BF16)


def _attn_call(q, kv, kr, gate, *, tq):
    b, s, _ = q.shape
    kernel = functools.partial(_attn_kernel, tq=tq)
    return pl.pallas_call(
        kernel,
        grid=(b, MLA_HEADS // 2),
        in_specs=[pl.BlockSpec((1, s, 512), lambda bi, hp: (bi, 0, hp)),
                  pl.BlockSpec((1, s, 512), lambda bi, hp: (bi, 0, hp)),
                  pl.BlockSpec((1, s, LANES), lambda bi, hp: (bi, 0, 0)),
                  pl.BlockSpec((1, s, 2 * V_DIM), lambda bi, hp: (bi, 0, hp))],
        out_specs=pl.BlockSpec((1, s, 2 * V_DIM), lambda bi, hp: (bi, 0, hp)),
        out_shape=jax.ShapeDtypeStruct((b, s, D_INNER), BF16),
        scratch_shapes=[pltpu.VMEM((2, s, 2 * LANES), BF16),
                        pltpu.VMEM((2, s, 2 * V_DIM), BF16)],
        compiler_params=_params(2),
        name="mla_flash_attention",
    )(q, kv, kr, gate)


def _rope_tables(seq):
    pos = jnp.arange(seq, dtype=F32)
    inv_freq = ROPE_THETA ** (-jnp.arange(0, ROPE_DIM, 2, dtype=F32) / ROPE_DIM)
    ang = pos[:, None] * inv_freq[None, :]
    cos, sin = jnp.cos(ang), jnp.sin(ang)
    cos4 = jnp.concatenate([cos, cos, cos, cos], axis=-1)
    sin4 = jnp.concatenate([-sin, sin, -sin, sin], axis=-1)
    return cos4, sin4


def _swap_halves(w):
    half = w.shape[-1] // 2
    return jnp.concatenate([w[..., half:], w[..., :half]], axis=-1)


def kernel(x, norm_g, hg_w_in, hg_g_norm, hg_w_out, hg_lb, kv_in_norm_g, w_kv_down, kv_norm_g,
           w_kv_up, mla_w_in, mla_q_norm_g, mla_w_q_up, mla_w_out, final_norm_g):
    b, s, d = x.shape
    m = b * s
    x2 = x.reshape(m, d)
    tm, tn, sub_n = MM_TM, MM_TN, MM_SUB_N
    row_spec = lambda n: pl.BlockSpec((tm, n), lambda i, j: (i, j))
    sblocks = s // tm
    rope_spec = pl.BlockSpec((tm, LANES), lambda i, j: (i % sblocks, 0))
    cos4, sin4 = _rope_tables(s)

    def project(xb, w, n_tile, epilogue, out_dtypes, name, extras=(), extra_specs=(), group=sub_n):
        n = w.shape[1]
        outs = _mm_call(xb, w.astype(BF16), tm=tm, tn=n_tile, sub_n=group, epilogue=epilogue,
                        extras=extras, extra_specs=extra_specs,
                        out_shapes=[jax.ShapeDtypeStruct((m, n), dt) for dt in out_dtypes],
                        out_specs=[row_spec(n_tile) for _ in out_dtypes], name=name)
        return outs if len(outs) > 1 else outs[0]

    xn = _norm_cast_call(x2, norm_g[0].reshape(1, d), tm=tm, name="hg_norm")
    w_in = hg_w_in[0]
    c1, c2, c3 = HG_KEY_TOTAL, 2 * HG_KEY_TOTAL, 2 * HG_KEY_TOTAL + D_INNER
    q = project(xn, w_in[:, :c1], tn, _ep_cast, [BF16], "hg_in_q")
    lf, kg = project(xn, w_in[:, c1:c2], tn // 2, _ep_forget_gate, [F32, BF16], "hg_in_f", extras=(hg_lb,),
                     extra_specs=(pl.BlockSpec((hg_lb.shape[0], tn // 2), lambda i, j: (0, j)),))
    vi = project(xn, w_in[:, c2:c3], tn, _ep_cast, [BF16], "hg_in_i")
    sg = project(xn, w_in[:, c3:], tn, _ep_silu, [BF16], "hg_in_g")

    y = _hgrn_call(q.reshape(b, s, -1), lf.reshape(b, s, -1), kg.reshape(b, s, -1),
                   vi.reshape(b, s, -1), sg.reshape(b, s, -1), hg_g_norm[0].reshape(1, HG_VAL_DIM),
                   hb=2, ts=512)
    h1, h1n_kv, h1n_q = _mm_res_call(y.reshape(m, D_INNER), hg_w_out[0].astype(BF16), x2,
                                     (kv_in_norm_g.reshape(1, d), norm_g[1].reshape(1, d)),
                                     final_norm=False, tm=RES_TM, sub_n=RES_SUB_N, name="hg_out")

    w_c, w_r = w_kv_down[:, :KV_LORA], w_kv_down[:, KV_LORA:]
    w_r_sw = _swap_halves(w_r)
    w_kvd = jnp.concatenate([w_c, w_r, w_r, w_r_sw, w_r_sw], axis=-1)
    n_kvd = w_kvd.shape[1]
    cn, kr = _mm_call(h1n_kv, w_kvd.astype(BF16), tm=tm, tn=n_kvd, sub_n=n_kvd, epilogue=_ep_kv_down,
                      extras=(kv_norm_g.reshape(1, KV_LORA), cos4, sin4),
                      extra_specs=(pl.BlockSpec((1, KV_LORA), lambda i, j: (0, 0)), rope_spec, rope_spec),
                      out_shapes=[jax.ShapeDtypeStruct((m, KV_LORA), BF16),
                                  jax.ShapeDtypeStruct((m, LANES), BF16)],
                      out_specs=[pl.BlockSpec((tm, KV_LORA), lambda i, j: (i, 0)),
                                 pl.BlockSpec((tm, LANES), lambda i, j: (i, 0))], name="kv_down")
    kv = project(cn, w_kv_up, tn, _ep_cast, [BF16], "kv_up")

    w_in1 = mla_w_in[0]
    cqn = _mm_call(h1n_q, w_in1[:, :Q_LORA].astype(BF16), tm=tm, tn=Q_LORA, sub_n=Q_LORA, epilogue=_ep_rms,
                   extras=(mla_q_norm_g[0].reshape(1, Q_LORA),),
                   extra_specs=(pl.BlockSpec((1, Q_LORA), lambda i, j: (0, 0)),),
                   out_shapes=[jax.ShapeDtypeStruct((m, Q_LORA), BF16)],
                   out_specs=[pl.BlockSpec((tm, Q_LORA), lambda i, j: (i, 0))], name="mla_in_q")[0]
    gate = project(h1n_q, w_in1[:, Q_LORA:], tn, _ep_silu, [BF16], "mla_in_gate")

    wq = mla_w_q_up[0].reshape(Q_LORA, MLA_HEADS, NOPE_DIM + ROPE_DIM)
    wq_nope = wq[:, :, :NOPE_DIM].reshape(Q_LORA, MLA_HEADS // 2, 2 * NOPE_DIM)
    wq_rope = wq[:, :, NOPE_DIM:]
    wq_rot = _swap_halves(wq_rope).reshape(Q_LORA, MLA_HEADS // 2, 2 * ROPE_DIM)
    wq_rope = wq_rope.reshape(Q_LORA, MLA_HEADS // 2, 2 * ROPE_DIM)
    wq_ext = jnp.concatenate([wq_nope, wq_rope, wq_rot], axis=-1).reshape(Q_LORA, -1)
    qf = project(cqn, wq_ext, tn, _ep_q_up, [BF16], "mla_q_up", extras=(cos4, sin4),
                 extra_specs=(rope_spec, rope_spec), group=512)

    attn = _attn_call(qf.reshape(b, s, -1), kv.reshape(b, s, -1), kr.reshape(b, s, -1),
                      gate.reshape(b, s, -1), tq=512)
    out = _mm_res_call(attn.reshape(m, D_INNER), mla_w_out[0].astype(BF16), h1,
                       (final_norm_g.reshape(1, d),), final_norm=True, tm=RES_TM, sub_n=RES_SUB_N,
                       name="mla_out")[0]
    return out.reshape(b, s, d)
```

```python
import functools

import numpy as np
import jax
import jax.numpy as jnp
from jax import lax
from jax.experimental import pallas as pl
from jax.experimental.pallas import tpu as pltpu

F32 = jnp.float32
BF16 = jnp.bfloat16

D_MODEL = 2048
D_INNER = 2 * D_MODEL
HG_KEY_DIM = 128
HG_HEADS = D_MODEL // HG_KEY_DIM
HG_KEY_TOTAL = HG_HEADS * HG_KEY_DIM
HG_VAL_DIM = D_INNER // HG_HEADS
HG_CHUNK = 64
HG_LEVELS = 6
MLA_HEADS = 32
Q_LORA = 768
KV_LORA = 512
NOPE_DIM = 128
ROPE_DIM = 64
V_DIM = 128
ROPE_THETA = 10000.0
EPS = 1e-6
LOG2_E = 1.4426950408889634

LANES = 128
VMEM_LIMIT = 56 * 1024 * 1024

MM_TM, MM_TN, MM_SUB_N = 2048, 1024, 512
RES_TM, RES_SUB_N = 512, 512

_NT = (((1,), (1,)), ((), ()))
_TN = (((0,), (0,)), ((), ()))


def _params(n_axes):
    return pltpu.CompilerParams(dimension_semantics=("arbitrary",) * n_axes,
                                vmem_limit_bytes=VMEM_LIMIT)


def _rms(x):
    return x * lax.rsqrt(jnp.mean(x * x, axis=-1, keepdims=True) + EPS)


def _silu(x):
    return x / (1.0 + jnp.exp(-x))


def _norm_cast_kernel(x_ref, g_ref, o_ref):
    o_ref[...] = (_rms(x_ref[...]) * g_ref[...]).astype(BF16)


def _norm_cast_call(x, g, *, tm, name):
    m, k = x.shape
    return pl.pallas_call(
        _norm_cast_kernel,
        grid=(m // tm,),
        in_specs=[pl.BlockSpec((tm, k), lambda i: (i, 0)), pl.BlockSpec((1, k), lambda i: (0, 0))],
        out_specs=pl.BlockSpec((tm, k), lambda i: (i, 0)),
        out_shape=jax.ShapeDtypeStruct((m, k), BF16),
        compiler_params=_params(1),
        name=name,
    )(x, g)


def _mm_kernel(epilogue, n_extra, sub_n, cast_w, x_ref, w_ref, *refs):
    extras = refs[:n_extra]
    if cast_w:
        outs, wb_ref = refs[n_extra:-1], refs[-1]

        @pl.when(pl.program_id(1) == 0)
        def _():
            wb_ref[...] = w_ref[...].astype(BF16)
    else:
        outs, wb_ref = refs[n_extra:], w_ref
    for c0 in range(0, wb_ref.shape[1], sub_n):
        cols = slice(c0, c0 + sub_n)
        acc = jnp.dot(x_ref[...], wb_ref[:, cols], preferred_element_type=F32)
        epilogue(acc, cols, extras, outs)


def _mm_call(x, w, *, n, col0=0, tm, tn, sub_n, epilogue, extras=(), extra_specs=(), out_shapes, out_specs,
             name):
    m, k = x.shape
    cast_w = w.dtype != BF16
    if col0 % tn == 0:
        w_spec = pl.BlockSpec((k, tn), lambda j, i: (0, j + col0 // tn))
    else:
        assert col0 % LANES == 0
        w_spec = pl.BlockSpec((pl.Element(k), pl.Element(tn)),
                              lambda j, i: (0, pl.multiple_of(col0 + j * tn, LANES)))
    kernel = functools.partial(_mm_kernel, epilogue, len(extras), sub_n, cast_w)
    return pl.pallas_call(
        kernel,
        grid=(n // tn, m // tm),
        in_specs=[pl.BlockSpec((tm, k), lambda j, i: (i, 0)), w_spec] + list(extra_specs),
        out_specs=out_specs,
        out_shape=out_shapes,
        scratch_shapes=[pltpu.VMEM((k, tn), BF16)] if cast_w else [],
        compiler_params=_params(2),
        name=name,
    )(x, w, *extras)


def _ep_cast(acc, cols, extras, outs):
    outs[0][:, cols] = acc.astype(outs[0].dtype)


def _ep_silu(acc, cols, extras, outs):
    outs[0][:, cols] = _silu(acc).astype(outs[0].dtype)


def _ep_forget_gate(acc, cols, extras, outs):
    lb_logits = extras[0][:, cols]
    mx = jnp.max(lb_logits, axis=0, keepdims=True)
    e = jnp.exp(lb_logits - mx)
    lb = e[0:1, :] / jnp.sum(e, axis=0, keepdims=True)
    t = jnp.exp(-jnp.abs(acc))
    r = 1.0 / (1.0 + t)
    pos = acc >= 0
    sig = jnp.where(pos, r, t * r)
    sig_neg = jnp.where(pos, t * r, r)
    outs[0][:, cols] = jnp.log(lb + (1.0 - lb) * sig)
    outs[1][:, cols] = ((1.0 - lb) * sig_neg).astype(BF16)


def _ep_rms(acc, cols, extras, outs):
    outs[0][...] = (_rms(acc) * extras[0][...]).astype(BF16)


def _ep_kv_down(acc, cols, extras, outs):
    gain, cos4, sin4 = extras
    outs[0][...] = (_rms(acc[:, :KV_LORA]) * gain[...]).astype(BF16)
    kr = acc[:, KV_LORA:KV_LORA + LANES] * cos4[...] + acc[:, KV_LORA + LANES:] * sin4[...]
    outs[1][...] = kr.astype(BF16)


def _ep_q_up(acc, cols, extras, outs):
    cos4, sin4 = extras
    scale = (NOPE_DIM + ROPE_DIM) ** -0.5 * LOG2_E
    lane = lax.broadcasted_iota(jnp.int32, (acc.shape[0], LANES), 1)
    first_half = lane < ROPE_DIM
    base = cols.start
    rope = (acc[:, 256:384] * cos4[...] + acc[:, 384:512] * sin4[...]) * scale
    outs[0][:, base:base + 128] = (acc[:, 0:128] * scale).astype(BF16)
    outs[0][:, base + 128:base + 256] = jnp.where(first_half, rope, 0.0).astype(BF16)
    outs[0][:, base + 256:base + 384] = (acc[:, 128:256] * scale).astype(BF16)
    outs[0][:, base + 384:base + 512] = jnp.where(first_half, 0.0, rope).astype(BF16)


def _mm_res_kernel(final_norm, sub_n, y_ref, w_ref, r_ref, *refs):
    if final_norm:
        g_ref, o_ref = refs
    else:
        ga_ref, gb_ref, o_ref, na_ref, nb_ref = refs
    n = w_ref.shape[1]
    ssq = jnp.zeros((y_ref.shape[0], 1), F32)
    for c0 in range(0, n, sub_n):
        cols = slice(c0, c0 + sub_n)
        h = r_ref[:, cols] + jnp.dot(y_ref[...], w_ref[:, cols], preferred_element_type=F32)
        o_ref[:, cols] = h
        ssq = ssq + jnp.sum(h * h, axis=-1, keepdims=True)
    inv = lax.rsqrt(ssq * (1.0 / n) + EPS)
    for c0 in range(0, n, sub_n):
        cols = slice(c0, c0 + sub_n)
        hn = o_ref[:, cols] * inv
        if final_norm:
            o_ref[:, cols] = hn * g_ref[:, cols]
        else:
            na_ref[:, cols] = (hn * ga_ref[:, cols]).astype(BF16)
            nb_ref[:, cols] = (hn * gb_ref[:, cols]).astype(BF16)


def _mm_res_call(y, w, res, gains, *, final_norm, tm, sub_n, name):
    m, kdim = y.shape
    n = w.shape[1]
    row_spec = pl.BlockSpec((tm, n), lambda i: (i, 0))
    gain_spec = pl.BlockSpec((1, n), lambda i: (0, 0))
    out_shape = [jax.ShapeDtypeStruct((m, n), F32)]
    out_specs = [row_spec]
    if not final_norm:
        out_shape += [jax.ShapeDtypeStruct((m, n), BF16)] * 2
        out_specs += [row_spec, row_spec]
    return pl.pallas_call(
        functools.partial(_mm_res_kernel, final_norm, sub_n),
        grid=(m // tm,),
        in_specs=[pl.BlockSpec((tm, kdim), lambda i: (i, 0)),
                  pl.BlockSpec((kdim, n), lambda i: (0, 0), pipeline_mode=pl.Buffered(1)),
                  row_spec] + [gain_spec] * len(gains),
        out_specs=out_specs,
        out_shape=out_shape,
        compiler_params=_params(1),
        name=name,
    )(y, w, res, *gains)


def _hgrn_exponent_matrix():
    c = HG_CHUNK
    t = np.arange(c)[:, None]
    u = np.arange(c)[None, :]
    mats = [(u <= t).astype(np.float32), (u > t).astype(np.float32)]
    for level in range(HG_LEVELS):
        m = 1 << level
        r = ((t >> (level + 1)) << (level + 1)) + m - 1
        upper = ((t >> level) & 1) == 1
        up = ((u > r) & (u <= t)).astype(np.float32)
        lo = ((u > t) & (u <= r)).astype(np.float32)
        mats.append(np.where(upper, up, lo))
    return np.concatenate(mats, axis=0)


def _hgrn_kernel(q_ref, lf_ref, k_ref, v_ref, sg_ref, gn_ref, e_ref, y_ref,
                 st_ref, ex_ref, qe_ref, a_ref, inc_ref, sb_ref, *, hb, ts):
    c = HG_CHUNK
    dk = HG_KEY_DIM
    dv = HG_VAL_DIM

    @pl.when(pl.program_id(2) == 0)
    def _():
        st_ref[...] = jnp.zeros_like(st_ref)

    nc = ts // c
    sub = 8
    e_mat = e_ref[...]
    t_i = lax.broadcasted_iota(jnp.int32, (c, c), 0)
    s_i = lax.broadcasted_iota(jnp.int32, (c, c), 1)
    diff = t_i ^ s_i
    causal = s_i < t_i
    pair_masks = [jnp.where(causal & ((diff >> level) == 1), 1.0, 0.0) for level in range(HG_LEVELS)]
    diag = jnp.where(t_i == s_i, 1.0, 0.0)
    row = lax.broadcasted_iota(jnp.int32, (c, dk), 0)
    uppers = [((row >> level) & 1) == 1 for level in range(HG_LEVELS)]
    gn = gn_ref[...]

    def decays(ci):
        lf = lf_ref[0, ci * c:(ci + 1) * c, :]
        hi = lf.astype(BF16)
        lo = (lf - hi.astype(F32)).astype(BF16)
        ex_ref[ci] = jnp.exp(jnp.dot(e_mat, jnp.concatenate([hi, lo], axis=0), preferred_element_type=F32))

    def chunk_local(ci):
        rows = slice(ci * c, (ci + 1) * c)
        for h in range(hb):
            idx = ci * hb + h
            cs = slice(h * dk, (h + 1) * dk)
            q = q_ref[0, rows, cs].astype(F32)
            k = k_ref[0, rows, cs].astype(F32)
            qe_ref[idx] = (q * ex_ref[ci, 0:c, cs]).astype(BF16)
            ks = (k * ex_ref[ci, c:2 * c, cs]).astype(BF16)
            inc_ref[idx] = lax.dot_general(v_ref[0, rows, h * dv:(h + 1) * dv], ks, _TN,
                                           preferred_element_type=F32)

            scores = diag * jnp.sum(q * k, axis=-1, keepdims=True)
            for level in range(HG_LEVELS):
                x = ex_ref[ci, (2 + level) * c:(3 + level) * c, cs]
                if (1 << level) >= sub:
                    w = jnp.concatenate(
                        [x[r0:r0 + sub] * (q if (r0 >> level) & 1 else k)[r0:r0 + sub] for r0 in range(0, c, sub)],
                        axis=0)
                else:
                    w = x * jnp.where(uppers[level], q, k)
                w = w.astype(BF16)
                scores = scores + lax.dot_general(w, w, _NT, preferred_element_type=F32) * pair_masks[level]
            a_ref[idx] = scores.astype(BF16)

    def recur(ci, states):
        new = []
        for h in range(hb):
            idx = ci * hb + h
            sb_ref[idx] = states[h].astype(BF16)
            new.append(states[h] * ex_ref[ci, c - 1:c, h * dk:(h + 1) * dk] + inc_ref[idx])
        return new

    def outputs(ci):
        rows = slice(ci * c, (ci + 1) * c)
        for h in range(hb):
            idx = ci * hb + h
            vs = slice(h * dv, (h + 1) * dv)
            o = (lax.dot_general(qe_ref[idx], sb_ref[idx], _NT, preferred_element_type=F32)
                 + jnp.dot(a_ref[idx], v_ref[0, rows, vs], preferred_element_type=F32))
            y = _rms(o) * gn * sg_ref[0, rows, vs].astype(F32)
            y_ref[0, rows, vs] = y.astype(BF16)

    states = [st_ref[h] for h in range(hb)]
    for it in range(nc + 2):
        if it < nc:
            decays(it)
        if 0 <= it - 1 < nc:
            chunk_local(it - 1)
            states = recur(it - 1, states)
        if 0 <= it - 2 < nc:
            outputs(it - 2)
    for h in range(hb):
        st_ref[h] = states[h]


def _hgrn_call(q, lf, k, v, sg, gn, *, hb, ts):
    b, s, _ = q.shape
    e_one = _hgrn_exponent_matrix()
    e_mat = jnp.asarray(np.concatenate([e_one, e_one], axis=1), dtype=BF16)
    kernel = functools.partial(_hgrn_kernel, hb=hb, ts=ts)
    nhc = (ts // HG_CHUNK) * hb
    key_spec = pl.BlockSpec((1, ts, hb * HG_KEY_DIM), lambda bi, hi, si: (bi, si, hi))
    val_spec = pl.BlockSpec((1, ts, hb * HG_VAL_DIM), lambda bi, hi, si: (bi, si, hi))
    return pl.pallas_call(
        kernel,
        grid=(b, HG_HEADS // hb, s // ts),
        in_specs=[key_spec, key_spec, key_spec, val_spec, val_spec,
                  pl.BlockSpec((1, HG_VAL_DIM), lambda bi, hi, si: (0, 0)),
                  pl.BlockSpec(e_mat.shape, lambda bi, hi, si: (0, 0))],
        out_specs=val_spec,
        out_shape=jax.ShapeDtypeStruct((b, s, D_INNER), BF16),
        scratch_shapes=[pltpu.VMEM((hb, HG_VAL_DIM, HG_KEY_DIM), F32),
                        pltpu.VMEM((ts // HG_CHUNK,) + (e_mat.shape[0], hb * HG_KEY_DIM), F32),
                        pltpu.VMEM((nhc, HG_CHUNK, HG_KEY_DIM), BF16),
                        pltpu.VMEM((nhc, HG_CHUNK, HG_CHUNK), BF16),
                        pltpu.VMEM((nhc, HG_VAL_DIM, HG_KEY_DIM), F32),
                        pltpu.VMEM((nhc, HG_VAL_DIM, HG_KEY_DIM), BF16)],
        compiler_params=_params(3),
        name="hgrn2_recurrence",
    )(q, lf, k, v, sg, gn, e_mat)


def _attn_kernel(q_ref, kv_ref, kr_ref, g_ref, o_ref, kcat_ref, vext_ref, *, tq):
    s_len = q_ref.shape[1]
    t_i = lax.broadcasted_iota(jnp.int32, (tq, tq), 0)
    s_i = lax.broadcasted_iota(jnp.int32, (tq, tq), 1)
    causal = s_i <= t_i

    for hh in range(2):
        kcat_ref[hh, :, :NOPE_DIM] = kv_ref[0, :, hh * 256:hh * 256 + NOPE_DIM]
        kcat_ref[hh, :, NOPE_DIM:] = kr_ref[0]
        vext_ref[hh, :, :V_DIM] = kv_ref[0, :, hh * 256 + NOPE_DIM:(hh + 1) * 256]
        vext_ref[hh, :, V_DIM:] = jnp.ones((s_len, V_DIM), BF16)

        for qi in range(s_len // tq):
            rows = slice(qi * tq, (qi + 1) * tq)
            q = q_ref[0, rows, hh * 256:(hh + 1) * 256]
            m = jnp.full((tq, 1), -jnp.inf, F32)
            acc = jnp.zeros((tq, 2 * V_DIM), F32)
            for kj in range(qi + 1):
                keys = slice(kj * tq, (kj + 1) * tq)
                sc = lax.dot_general(q, kcat_ref[hh, keys, :], _NT, preferred_element_type=F32)
                if kj == qi:
                    sc = jnp.where(causal, sc, -jnp.inf)
                m_new = jnp.maximum(m, jnp.max(sc, axis=-1, keepdims=True))
                p = jnp.exp2(sc - m_new)
                acc = jnp.exp2(m - m_new) * acc + jnp.dot(p.astype(BF16), vext_ref[hh, keys, :],
                                                          preferred_element_type=F32)
                m = m_new
            out = acc[:, :V_DIM] / acc[:, V_DIM:] * g_ref[0, rows, hh * V_DIM:(hh + 1) * V_DIM].astype(F32)
            o_ref[0, rows, hh * V_DIM:(hh + 1) * V_DIM] = out.astype(BF16)


def _attn_call(q, kv, kr, gate, *, tq):
    b, s, _ = q.shape
    kernel = functools.partial(_attn_kernel, tq=tq)
    return pl.pallas_call(
        kernel,
        grid=(b, MLA_HEADS // 2),
        in_specs=[pl.BlockSpec((1, s, 512), lambda bi, hp: (bi, 0, hp)),
                  pl.BlockSpec((1, s, 512), lambda bi, hp: (bi, 0, hp)),
                  pl.BlockSpec((1, s, LANES), lambda bi, hp: (bi, 0, 0)),
                  pl.BlockSpec((1, s, 2 * V_DIM), lambda bi, hp: (bi, 0, hp))],
        out_specs=pl.BlockSpec((1, s, 2 * V_DIM), lambda bi, hp: (bi, 0, hp)),
        out_shape=jax.ShapeDtypeStruct((b, s, D_INNER), BF16),
        scratch_shapes=[pltpu.VMEM((2, s, 2 * LANES), BF16),
                        pltpu.VMEM((2, s, 2 * V_DIM), BF16)],
        compiler_params=_params(2),
        name="mla_flash_attention",
    )(q, kv, kr, gate)


def _rope_tables(seq):
    pos = np.arange(seq, dtype=np.float64)
    inv_freq = ROPE_THETA ** (-np.arange(0, ROPE_DIM, 2, dtype=np.float64) / ROPE_DIM)
    ang = pos[:, None] * inv_freq[None, :]
    cos, sin = np.cos(ang), np.sin(ang)
    cos4 = np.concatenate([cos, cos, cos, cos], axis=-1)
    sin4 = np.concatenate([-sin, sin, -sin, sin], axis=-1)
    return jnp.asarray(cos4, dtype=F32), jnp.asarray(sin4, dtype=F32)


def _swap_halves(w):
    half = w.shape[-1] // 2
    return jnp.concatenate([w[..., half:], w[..., :half]], axis=-1)


def kernel(x, norm_g, hg_w_in, hg_g_norm, hg_w_out, hg_lb, kv_in_norm_g, w_kv_down, kv_norm_g,
           w_kv_up, mla_w_in, mla_q_norm_g, mla_w_q_up, mla_w_out, final_norm_g):
    b, s, d = x.shape
    m = b * s
    x2 = x.reshape(m, d)
    tm, tn, sub_n = MM_TM, MM_TN, MM_SUB_N
    cos4, sin4 = _rope_tables(s)

    def rope_spec(rows):
        return pl.BlockSpec((rows, LANES), lambda j, i: (i % (s // rows), 0))

    def project(xb, w, n, epilogue, out_dtypes, name, *, col0=0, rows=tm, n_tile=tn, group=sub_n,
                extras=(), extra_specs=()):
        outs = _mm_call(xb, w, n=n, col0=col0, tm=rows, tn=n_tile, sub_n=group, epilogue=epilogue,
                        extras=extras, extra_specs=extra_specs,
                        out_shapes=[jax.ShapeDtypeStruct((m, n), dt) for dt in out_dtypes],
                        out_specs=[pl.BlockSpec((rows, n_tile), lambda j, i: (i, j)) for _ in out_dtypes],
                        name=name)
        return outs if len(outs) > 1 else outs[0]

    xn = _norm_cast_call(x2, norm_g[0].reshape(1, d), tm=1024, name="hg_norm")
    w_in = hg_w_in[0]
    c1, c2, c3 = HG_KEY_TOTAL, 2 * HG_KEY_TOTAL, 2 * HG_KEY_TOTAL + D_INNER
    q = project(xn, w_in, c1, _ep_cast, [BF16], "hg_in_q")
    lf, kg = project(xn, w_in, c2 - c1, _ep_forget_gate, [F32, BF16], "hg_in_f", col0=c1, rows=tm // 2,
                     extras=(hg_lb,),
                     extra_specs=(pl.BlockSpec((hg_lb.shape[0], tn), lambda j, i: (0, j)),))
    vi = project(xn, w_in, c3 - c2, _ep_cast, [BF16], "hg_in_i", col0=c2)
    sg = project(xn, w_in, w_in.shape[1] - c3, _ep_silu, [BF16], "hg_in_g", col0=c3)

    y = _hgrn_call(q.reshape(b, s, -1), lf.reshape(b, s, -1), kg.reshape(b, s, -1),
                   vi.reshape(b, s, -1), sg.reshape(b, s, -1), hg_g_norm[0].reshape(1, HG_VAL_DIM),
                   hb=2, ts=512)
    h1, h1n_kv, h1n_q = _mm_res_call(y.reshape(m, D_INNER), hg_w_out[0].astype(BF16), x2,
                                     (kv_in_norm_g.reshape(1, d), norm_g[1].reshape(1, d)),
                                     final_norm=False, tm=RES_TM, sub_n=RES_SUB_N, name="hg_out")

    w_c, w_r = w_kv_down[:, :KV_LORA], w_kv_down[:, KV_LORA:]
    w_r_sw = _swap_halves(w_r)
    w_kvd = jnp.concatenate([w_c, w_r, w_r, w_r_sw, w_r_sw], axis=-1)
    n_kvd = w_kvd.shape[1]
    cn, kr = _mm_call(h1n_kv, w_kvd, n=n_kvd, tm=1024, tn=n_kvd, sub_n=n_kvd, epilogue=_ep_kv_down,
                      extras=(kv_norm_g.reshape(1, KV_LORA), cos4, sin4),
                      extra_specs=(pl.BlockSpec((1, KV_LORA), lambda j, i: (0, 0)), rope_spec(1024),
                                   rope_spec(1024)),
                      out_shapes=[jax.ShapeDtypeStruct((m, KV_LORA), BF16),
                                  jax.ShapeDtypeStruct((m, LANES), BF16)],
                      out_specs=[pl.BlockSpec((1024, KV_LORA), lambda j, i: (i, 0)),
                                 pl.BlockSpec((1024, LANES), lambda j, i: (i, 0))], name="kv_down")
    kv = project(cn, w_kv_up, w_kv_up.shape[1], _ep_cast, [BF16], "kv_up", n_tile=2 * tn)

    w_in1 = mla_w_in[0]
    cqn = _mm_call(h1n_q, w_in1, n=Q_LORA, tm=1024, tn=Q_LORA, sub_n=Q_LORA, epilogue=_ep_rms,
                   extras=(mla_q_norm_g[0].reshape(1, Q_LORA),),
                   extra_specs=(pl.BlockSpec((1, Q_LORA), lambda j, i: (0, 0)),),
                   out_shapes=[jax.ShapeDtypeStruct((m, Q_LORA), BF16)],
                   out_specs=[pl.BlockSpec((1024, Q_LORA), lambda j, i: (i, 0))], name="mla_in_q")[0]
    gate = project(h1n_q, w_in1, w_in1.shape[1] - Q_LORA, _ep_silu, [BF16], "mla_in_gate", col0=Q_LORA)

    wq = mla_w_q_up[0].reshape(Q_LORA, MLA_HEADS, NOPE_DIM + ROPE_DIM)
    wq_nope = wq[:, :, :NOPE_DIM].reshape(Q_LORA, MLA_HEADS // 2, 2 * NOPE_DIM)
    wq_rope = wq[:, :, NOPE_DIM:]
    wq_rot = _swap_halves(wq_rope).reshape(Q_LORA, MLA_HEADS // 2, 2 * ROPE_DIM)
    wq_rope = wq_rope.reshape(Q_LORA, MLA_HEADS // 2, 2 * ROPE_DIM)
    wq_ext = jnp.concatenate([wq_nope, wq_rope, wq_rot], axis=-1).reshape(Q_LORA, -1)
    wq_ext = wq_ext.astype(BF16)
    qf = project(cqn, wq_ext, wq_ext.shape[1], _ep_q_up, [BF16], "mla_q_up", n_tile=2 * tn, group=512,
                 extras=(cos4, sin4), extra_specs=(rope_spec(tm), rope_spec(tm)))

    attn = _attn_call(qf.reshape(b, s, -1), kv.reshape(b, s, -1), kr.reshape(b, s, -1),
                      gate.reshape(b, s, -1), tq=512)
    out = _mm_res_call(attn.reshape(m, D_INNER), mla_w_out[0].astype(BF16), h1,
                       (final_norm_g.reshape(1, d),), final_norm=True, tm=RES_TM, sub_n=RES_SUB_N,
                       name="mla_out")[0]
    return out.reshape(b, s, d)
```

```python
import functools

import numpy as np
import jax
import jax.numpy as jnp
from jax import lax
from jax.experimental import pallas as pl
from jax.experimental.pallas import tpu as pltpu

F32 = jnp.float32
BF16 = jnp.bfloat16

D_MODEL = 2048
D_INNER = 2 * D_MODEL
HG_KEY_DIM = 128
HG_HEADS = D_MODEL // HG_KEY_DIM
HG_KEY_TOTAL = HG_HEADS * HG_KEY_DIM
HG_VAL_DIM = D_INNER // HG_HEADS
HG_CHUNK = 64
HG_LEVELS = 6
MLA_HEADS = 32
Q_LORA = 768
KV_LORA = 512
NOPE_DIM = 128
ROPE_DIM = 64
V_DIM = 128
ROPE_THETA = 10000.0
EPS = 1e-6
LOG2_E = 1.4426950408889634

LANES = 128
VMEM_LIMIT = 56 * 1024 * 1024

MM_TM, MM_TN, MM_SUB_M, MM_SUB_N = 2048, 1024, 128, 512
CAST_ROWS = 256
RES_TM, RES_SUB_N = 512, 512

_NT = (((1,), (1,)), ((), ()))
_TN = (((0,), (0,)), ((), ()))


def _params(n_axes):
    return pltpu.CompilerParams(dimension_semantics=("arbitrary",) * n_axes,
                                vmem_limit_bytes=VMEM_LIMIT)


def _rms(x):
    return x * lax.rsqrt(jnp.mean(x * x, axis=-1, keepdims=True) + EPS)


def _silu(x):
    return x / (1.0 + jnp.exp(-x))


def _norm_cast_kernel(x_ref, g_ref, o_ref):
    o_ref[...] = (_rms(x_ref[...]) * g_ref[...]).astype(BF16)


def _norm_cast_call(x, g, *, tm, name):
    m, k = x.shape
    return pl.pallas_call(
        _norm_cast_kernel,
        grid=(m // tm,),
        in_specs=[pl.BlockSpec((tm, k), lambda i: (i, 0)), pl.BlockSpec((1, k), lambda i: (0, 0))],
        out_specs=pl.BlockSpec((tm, k), lambda i: (i, 0)),
        out_shape=jax.ShapeDtypeStruct((m, k), BF16),
        compiler_params=_params(1),
        name=name,
    )(x, g)


def _mm_kernel(epilogue, n_extra, sub_m, sub_n, cast_w, x_ref, w_ref, *refs):
    extras = refs[:n_extra]
    if cast_w:
        outs, wb_ref = refs[n_extra:-1], refs[-1]

        @pl.when(pl.program_id(1) == 0)
        def _():
            def cast_rows(kk, carry):
                rows = pl.ds(pl.multiple_of(kk * CAST_ROWS, CAST_ROWS), CAST_ROWS)
                wb_ref[rows, :] = w_ref[rows, :].astype(BF16)
                return carry
            lax.fori_loop(0, w_ref.shape[0] // CAST_ROWS, cast_rows, 0)
    else:
        outs, wb_ref = refs[n_extra:], w_ref
    for c0 in range(0, wb_ref.shape[1], sub_n):
        cols = slice(c0, c0 + sub_n)
        for r0 in range(0, x_ref.shape[0], sub_m):
            rows = slice(r0, r0 + sub_m)
            acc = jnp.dot(x_ref[rows, :], wb_ref[:, cols], preferred_element_type=F32)
            epilogue(acc, rows, cols, extras, outs)


def _mm_call(x, w, *, n, col0=0, tm, tn, sub_m, sub_n, epilogue, extras=(), extra_specs=(), out_shapes,
             out_specs, name):
    m, k = x.shape
    cast_w = w.dtype != BF16
    if col0 % tn == 0:
        w_spec = pl.BlockSpec((k, tn), lambda j, i: (0, j + col0 // tn))
    else:
        assert col0 % LANES == 0
        w_spec = pl.BlockSpec((pl.Element(k), pl.Element(tn)),
                              lambda j, i: (0, pl.multiple_of(col0 + j * tn, LANES)))
    kernel = functools.partial(_mm_kernel, epilogue, len(extras), sub_m, sub_n, cast_w)
    return pl.pallas_call(
        kernel,
        grid=(n // tn, m // tm),
        in_specs=[pl.BlockSpec((tm, k), lambda j, i: (i, 0)), w_spec] + list(extra_specs),
        out_specs=out_specs,
        out_shape=out_shapes,
        scratch_shapes=[pltpu.VMEM((k, tn), BF16)] if cast_w else [],
        compiler_params=_params(2),
        name=name,
    )(x, w, *extras)


def _ep_cast(acc, rows, cols, extras, outs):
    outs[0][rows, cols] = acc.astype(outs[0].dtype)


def _ep_silu(acc, rows, cols, extras, outs):
    outs[0][rows, cols] = _silu(acc).astype(outs[0].dtype)


def _ep_forget_gate(acc, rows, cols, extras, outs):
    lb_logits = extras[0][:, cols]
    mx = jnp.max(lb_logits, axis=0, keepdims=True)
    e = jnp.exp(lb_logits - mx)
    lb = e[0:1, :] / jnp.sum(e, axis=0, keepdims=True)
    t = jnp.exp(-jnp.abs(acc))
    r = 1.0 / (1.0 + t)
    pos = acc >= 0
    sig = jnp.where(pos, r, t * r)
    sig_neg = jnp.where(pos, t * r, r)
    outs[0][rows, cols] = jnp.log(lb + (1.0 - lb) * sig)
    outs[1][rows, cols] = ((1.0 - lb) * sig_neg).astype(BF16)


def _ep_rms(acc, rows, cols, extras, outs):
    outs[0][rows, :] = (_rms(acc) * extras[0][...]).astype(BF16)


def _ep_kv_down(acc, rows, cols, extras, outs):
    gain, cos4, sin4 = extras
    outs[0][rows, :] = (_rms(acc[:, :KV_LORA]) * gain[...]).astype(BF16)
    kr = acc[:, KV_LORA:KV_LORA + LANES] * cos4[rows, :] + acc[:, KV_LORA + LANES:] * sin4[rows, :]
    outs[1][rows, :] = kr.astype(BF16)


def _ep_q_up(acc, rows, cols, extras, outs):
    cos4, sin4 = extras
    scale = (NOPE_DIM + ROPE_DIM) ** -0.5 * LOG2_E
    lane = lax.broadcasted_iota(jnp.int32, (acc.shape[0], LANES), 1)
    first_half = lane < ROPE_DIM
    base = cols.start
    rope = (acc[:, 256:384] * cos4[rows, :] + acc[:, 384:512] * sin4[rows, :]) * scale
    outs[0][rows, base:base + 128] = (acc[:, 0:128] * scale).astype(BF16)
    outs[0][rows, base + 128:base + 256] = jnp.where(first_half, rope, 0.0).astype(BF16)
    outs[0][rows, base + 256:base + 384] = (acc[:, 128:256] * scale).astype(BF16)
    outs[0][rows, base + 384:base + 512] = jnp.where(first_half, 0.0, rope).astype(BF16)


def _mm_res_kernel(final_norm, sub_n, y_ref, w_ref, r_ref, *refs):
    if final_norm:
        g_ref, o_ref = refs
    else:
        ga_ref, gb_ref, o_ref, na_ref, nb_ref = refs
    n = w_ref.shape[1]
    ssq = jnp.zeros((y_ref.shape[0], 1), F32)
    for c0 in range(0, n, sub_n):
        cols = slice(c0, c0 + sub_n)
        h = r_ref[:, cols] + jnp.dot(y_ref[...], w_ref[:, cols], preferred_element_type=F32)
        o_ref[:, cols] = h
        ssq = ssq + jnp.sum(h * h, axis=-1, keepdims=True)
    inv = lax.rsqrt(ssq * (1.0 / n) + EPS)
    for c0 in range(0, n, sub_n):
        cols = slice(c0, c0 + sub_n)
        hn = o_ref[:, cols] * inv
        if final_norm:
            o_ref[:, cols] = hn * g_ref[:, cols]
        else:
            na_ref[:, cols] = (hn * ga_ref[:, cols]).astype(BF16)
            nb_ref[:, cols] = (hn * gb_ref[:, cols]).astype(BF16)


def _mm_res_call(y, w, res, gains, *, final_norm, tm, sub_n, name):
    m, kdim = y.shape
    n = w.shape[1]
    row_spec = pl.BlockSpec((tm, n), lambda i: (i, 0))
    gain_spec = pl.BlockSpec((1, n), lambda i: (0, 0))
    out_shape = [jax.ShapeDtypeStruct((m, n), F32)]
    out_specs = [row_spec]
    if not final_norm:
        out_shape += [jax.ShapeDtypeStruct((m, n), BF16)] * 2
        out_specs += [row_spec, row_spec]
    return pl.pallas_call(
        functools.partial(_mm_res_kernel, final_norm, sub_n),
        grid=(m // tm,),
        in_specs=[pl.BlockSpec((tm, kdim), lambda i: (i, 0)),
                  pl.BlockSpec((kdim, n), lambda i: (0, 0), pipeline_mode=pl.Buffered(1)),
                  row_spec] + [gain_spec] * len(gains),
        out_specs=out_specs,
        out_shape=out_shape,
        compiler_params=_params(1),
        name=name,
    )(y, w, res, *gains)


def _hgrn_exponent_matrix():
    c = HG_CHUNK
    t = np.arange(c)[:, None]
    u = np.arange(c)[None, :]
    mats = [(u <= t).astype(np.float32), (u > t).astype(np.float32)]
    for level in range(HG_LEVELS):
        m = 1 << level
        r = ((t >> (level + 1)) << (level + 1)) + m - 1
        upper = ((t >> level) & 1) == 1
        up = ((u > r) & (u <= t)).astype(np.float32)
        lo = ((u > t) & (u <= r)).astype(np.float32)
        mats.append(np.where(upper, up, lo))
    return np.concatenate(mats, axis=0)


def _hgrn_kernel(q_ref, lf_ref, k_ref, v_ref, sg_ref, gn_ref, e_ref, y_ref,
                 st_ref, ex_ref, qe_ref, a_ref, inc_ref, sb_ref, *, hb, ts):
    c = HG_CHUNK
    dk = HG_KEY_DIM
    dv = HG_VAL_DIM

    @pl.when(pl.program_id(2) == 0)
    def _():
        st_ref[...] = jnp.zeros_like(st_ref)

    nc = ts // c
    sub = 8
    e_mat = e_ref[...]
    t_i = lax.broadcasted_iota(jnp.int32, (c, c), 0)
    s_i = lax.broadcasted_iota(jnp.int32, (c, c), 1)
    diff = t_i ^ s_i
    causal = s_i < t_i
    pair_masks = [jnp.where(causal & ((diff >> level) == 1), 1.0, 0.0) for level in range(HG_LEVELS)]
    diag = jnp.where(t_i == s_i, 1.0, 0.0)
    row = lax.broadcasted_iota(jnp.int32, (c, dk), 0)
    uppers = [((row >> level) & 1) == 1 for level in range(HG_LEVELS)]
    gn = gn_ref[...]

    def decays(ci):
        lf = lf_ref[0, ci * c:(ci + 1) * c, :]
        hi = lf.astype(BF16)
        lo = (lf - hi.astype(F32)).astype(BF16)
        ex_ref[ci] = jnp.exp(jnp.dot(e_mat, jnp.concatenate([hi, lo], axis=0), preferred_element_type=F32))

    def chunk_local(ci):
        rows = slice(ci * c, (ci + 1) * c)
        for h in range(hb):
            idx = ci * hb + h
            cs = slice(h * dk, (h + 1) * dk)
            q = q_ref[0, rows, cs].astype(F32)
            k = k_ref[0, rows, cs].astype(F32)
            qe_ref[idx] = (q * ex_ref[ci, 0:c, cs]).astype(BF16)
            ks = (k * ex_ref[ci, c:2 * c, cs]).astype(BF16)
            inc_ref[idx] = lax.dot_general(v_ref[0, rows, h * dv:(h + 1) * dv], ks, _TN,
                                           preferred_element_type=F32)

            scores = diag * jnp.sum(q * k, axis=-1, keepdims=True)
            for level in range(HG_LEVELS):
                x = ex_ref[ci, (2 + level) * c:(3 + level) * c, cs]
                if (1 << level) >= sub:
                    w = jnp.concatenate(
                        [x[r0:r0 + sub] * (q if (r0 >> level) & 1 else k)[r0:r0 + sub] for r0 in range(0, c, sub)],
                        axis=0)
                else:
                    w = x * jnp.where(uppers[level], q, k)
                w = w.astype(BF16)
                scores = scores + lax.dot_general(w, w, _NT, preferred_element_type=F32) * pair_masks[level]
            a_ref[idx] = scores.astype(BF16)

    def recur(ci, states):
        new = []
        for h in range(hb):
            idx = ci * hb + h
            sb_ref[idx] = states[h].astype(BF16)
            new.append(states[h] * ex_ref[ci, c - 1:c, h * dk:(h + 1) * dk] + inc_ref[idx])
        return new

    def outputs(ci):
        rows = slice(ci * c, (ci + 1) * c)
        for h in range(hb):
            idx = ci * hb + h
            vs = slice(h * dv, (h + 1) * dv)
            o = (lax.dot_general(qe_ref[idx], sb_ref[idx], _NT, preferred_element_type=F32)
                 + jnp.dot(a_ref[idx], v_ref[0, rows, vs], preferred_element_type=F32))
            y = _rms(o) * gn * sg_ref[0, rows, vs].astype(F32)
            y_ref[0, rows, vs] = y.astype(BF16)

    states = [st_ref[h] for h in range(hb)]
    for it in range(nc + 2):
        if it < nc:
            decays(it)
        if 0 <= it - 1 < nc:
            chunk_local(it - 1)
            states = recur(it - 1, states)
        if 0 <= it - 2 < nc:
            outputs(it - 2)
    for h in range(hb):
        st_ref[h] = states[h]


def _hgrn_call(q, lf, k, v, sg, gn, *, hb, ts):
    b, s, _ = q.shape
    e_one = _hgrn_exponent_matrix()
    e_mat = jnp.asarray(np.concatenate([e_one, e_one], axis=1), dtype=BF16)
    kernel = functools.partial(_hgrn_kernel, hb=hb, ts=ts)
    nhc = (ts // HG_CHUNK) * hb
    key_spec = pl.BlockSpec((1, ts, hb * HG_KEY_DIM), lambda bi, hi, si: (bi, si, hi))
    val_spec = pl.BlockSpec((1, ts, hb * HG_VAL_DIM), lambda bi, hi, si: (bi, si, hi))
    return pl.pallas_call(
        kernel,
        grid=(b, HG_HEADS // hb, s // ts),
        in_specs=[key_spec, key_spec, key_spec, val_spec, val_spec,
                  pl.BlockSpec((1, HG_VAL_DIM), lambda bi, hi, si: (0, 0)),
                  pl.BlockSpec(e_mat.shape, lambda bi, hi, si: (0, 0))],
        out_specs=val_spec,
        out_shape=jax.ShapeDtypeStruct((b, s, D_INNER), BF16),
        scratch_shapes=[pltpu.VMEM((hb, HG_VAL_DIM, HG_KEY_DIM), F32),
                        pltpu.VMEM((ts // HG_CHUNK,) + (e_mat.shape[0], hb * HG_KEY_DIM), F32),
                        pltpu.VMEM((nhc, HG_CHUNK, HG_KEY_DIM), BF16),
                        pltpu.VMEM((nhc, HG_CHUNK, HG_CHUNK), BF16),
                        pltpu.VMEM((nhc, HG_VAL_DIM, HG_KEY_DIM), F32),
                        pltpu.VMEM((nhc, HG_VAL_DIM, HG_KEY_DIM), BF16)],
        compiler_params=_params(3),
        name="hgrn2_recurrence",
    )(q, lf, k, v, sg, gn, e_mat)


def _attn_kernel(q_ref, kv_ref, kr_ref, g_ref, o_ref, kcat_ref, vext_ref, *, tq):
    s_len = q_ref.shape[1]
    t_i = lax.broadcasted_iota(jnp.int32, (tq, tq), 0)
    s_i = lax.broadcasted_iota(jnp.int32, (tq, tq), 1)
    causal = s_i <= t_i

    for hh in range(2):
        kcat_ref[hh, :, :NOPE_DIM] = kv_ref[0, :, hh * 256:hh * 256 + NOPE_DIM]
        kcat_ref[hh, :, NOPE_DIM:] = kr_ref[0]
        vext_ref[hh, :, :V_DIM] = kv_ref[0, :, hh * 256 + NOPE_DIM:(hh + 1) * 256]
        vext_ref[hh, :, V_DIM:] = jnp.ones((s_len, V_DIM), BF16)

        for qi in range(s_len // tq):
            rows = slice(qi * tq, (qi + 1) * tq)
            q = q_ref[0, rows, hh * 256:(hh + 1) * 256]
            m = jnp.full((tq, 1), -jnp.inf, F32)
            acc = jnp.zeros((tq, 2 * V_DIM), F32)
            for kj in range(qi + 1):
                keys = slice(kj * tq, (kj + 1) * tq)
                sc = lax.dot_general(q, kcat_ref[hh, keys, :], _NT, preferred_element_type=F32)
                if kj == qi:
                    sc = jnp.where(causal, sc, -jnp.inf)
                m_new = jnp.maximum(m, jnp.max(sc, axis=-1, keepdims=True))
                p = jnp.exp2(sc - m_new)
                acc = jnp.exp2(m - m_new) * acc + jnp.dot(p.astype(BF16), vext_ref[hh, keys, :],
                                                          preferred_element_type=F32)
                m = m_new
            out = acc[:, :V_DIM] / acc[:, V_DIM:] * g_ref[0, rows, hh * V_DIM:(hh + 1) * V_DIM].astype(F32)
            o_ref[0, rows, hh * V_DIM:(hh + 1) * V_DIM] = out.astype(BF16)


def _attn_call(q, kv, kr, gate, *, tq):
    b, s, _ = q.shape
    kernel = functools.partial(_attn_kernel, tq=tq)
    return pl.pallas_call(
        kernel,
        grid=(b, MLA_HEADS // 2),
        in_specs=[pl.BlockSpec((1, s, 512), lambda bi, hp: (bi, 0, hp)),
                  pl.BlockSpec((1, s, 512), lambda bi, hp: (bi, 0, hp)),
                  pl.BlockSpec((1, s, LANES), lambda bi, hp: (bi, 0, 0)),
                  pl.BlockSpec((1, s, 2 * V_DIM), lambda bi, hp: (bi, 0, hp))],
        out_specs=pl.BlockSpec((1, s, 2 * V_DIM), lambda bi, hp: (bi, 0, hp)),
        out_shape=jax.ShapeDtypeStruct((b, s, D_INNER), BF16),
        scratch_shapes=[pltpu.VMEM((2, s, 2 * LANES), BF16),
                        pltpu.VMEM((2, s, 2 * V_DIM), BF16)],
        compiler_params=_params(2),
        name="mla_flash_attention",
    )(q, kv, kr, gate)


def _rope_tables(seq):
    pos = np.arange(seq, dtype=np.float64)
    inv_freq = ROPE_THETA ** (-np.arange(0, ROPE_DIM, 2, dtype=np.float64) / ROPE_DIM)
    ang = pos[:, None] * inv_freq[None, :]
    cos, sin = np.cos(ang), np.sin(ang)
    cos4 = np.concatenate([cos, cos, cos, cos], axis=-1)
    sin4 = np.concatenate([-sin, sin, -sin, sin], axis=-1)
    return jnp.asarray(cos4, dtype=F32), jnp.asarray(sin4, dtype=F32)


def _swap_halves(w):
    half = w.shape[-1] // 2
    return jnp.concatenate([w[..., half:], w[..., :half]], axis=-1)


def kernel(x, norm_g, hg_w_in, hg_g_norm, hg_w_out, hg_lb, kv_in_norm_g, w_kv_down, kv_norm_g,
           w_kv_up, mla_w_in, mla_q_norm_g, mla_w_q_up, mla_w_out, final_norm_g):
    b, s, d = x.shape
    m = b * s
    x2 = x.reshape(m, d)
    tm, tn, sub_n = MM_TM, MM_TN, MM_SUB_N
    cos4, sin4 = _rope_tables(s)

    def rope_spec(rows):
        return pl.BlockSpec((rows, LANES), lambda j, i: (i % (s // rows), 0))

    def project(xb, w, n, epilogue, out_dtypes, name, *, col0=0, rows=tm, n_tile=tn, group=sub_n,
                extras=(), extra_specs=()):
        outs = _mm_call(xb, w, n=n, col0=col0, tm=rows, tn=n_tile, sub_m=MM_SUB_M, sub_n=group,
                        epilogue=epilogue,
                        extras=extras, extra_specs=extra_specs,
                        out_shapes=[jax.ShapeDtypeStruct((m, n), dt) for dt in out_dtypes],
                        out_specs=[pl.BlockSpec((rows, n_tile), lambda j, i: (i, j)) for _ in out_dtypes],
                        name=name)
        return outs if len(outs) > 1 else outs[0]

    xn = _norm_cast_call(x2, norm_g[0].reshape(1, d), tm=1024, name="hg_norm")
    w_in = hg_w_in[0]
    c1, c2, c3 = HG_KEY_TOTAL, 2 * HG_KEY_TOTAL, 2 * HG_KEY_TOTAL + D_INNER
    q = project(xn, w_in, c1, _ep_cast, [BF16], "hg_in_q")
    lf, kg = project(xn, w_in, c2 - c1, _ep_forget_gate, [F32, BF16], "hg_in_f", col0=c1, rows=tm // 2,
                     extras=(hg_lb,),
                     extra_specs=(pl.BlockSpec((hg_lb.shape[0], tn), lambda j, i: (0, j)),))
    vi = project(xn, w_in, c3 - c2, _ep_cast, [BF16], "hg_in_i", col0=c2)
    sg = project(xn, w_in, w_in.shape[1] - c3, _ep_silu, [BF16], "hg_in_g", col0=c3)

    y = _hgrn_call(q.reshape(b, s, -1), lf.reshape(b, s, -1), kg.reshape(b, s, -1),
                   vi.reshape(b, s, -1), sg.reshape(b, s, -1), hg_g_norm[0].reshape(1, HG_VAL_DIM),
                   hb=2, ts=512)
    h1, h1n_kv, h1n_q = _mm_res_call(y.reshape(m, D_INNER), hg_w_out[0].astype(BF16), x2,
                                     (kv_in_norm_g.reshape(1, d), norm_g[1].reshape(1, d)),
                                     final_norm=False, tm=RES_TM, sub_n=RES_SUB_N, name="hg_out")

    w_c, w_r = w_kv_down[:, :KV_LORA], w_kv_down[:, KV_LORA:]
    w_r_sw = _swap_halves(w_r)
    w_kvd = jnp.concatenate([w_c, w_r, w_r, w_r_sw, w_r_sw], axis=-1)
    n_kvd = w_kvd.shape[1]
    cn, kr = _mm_call(h1n_kv, w_kvd, n=n_kvd, tm=1024, tn=n_kvd, sub_m=MM_SUB_M, sub_n=n_kvd,
                      epilogue=_ep_kv_down,
                      extras=(kv_norm_g.reshape(1, KV_LORA), cos4, sin4),
                      extra_specs=(pl.BlockSpec((1, KV_LORA), lambda j, i: (0, 0)), rope_spec(1024),
                                   rope_spec(1024)),
                      out_shapes=[jax.ShapeDtypeStruct((m, KV_LORA), BF16),
                                  jax.ShapeDtypeStruct((m, LANES), BF16)],
                      out_specs=[pl.BlockSpec((1024, KV_LORA), lambda j, i: (i, 0)),
                                 pl.BlockSpec((1024, LANES), lambda j, i: (i, 0))], name="kv_down")
    kv = project(cn, w_kv_up, w_kv_up.shape[1], _ep_cast, [BF16], "kv_up", n_tile=2 * tn)

    w_in1 = mla_w_in[0]
    cqn = _mm_call(h1n_q, w_in1, n=Q_LORA, tm=1024, tn=Q_LORA, sub_m=MM_SUB_M, sub_n=Q_LORA,
                   epilogue=_ep_rms,
                   extras=(mla_q_norm_g[0].reshape(1, Q_LORA),),
                   extra_specs=(pl.BlockSpec((1, Q_LORA), lambda j, i: (0, 0)),),
                   out_shapes=[jax.ShapeDtypeStruct((m, Q_LORA), BF16)],
                   out_specs=[pl.BlockSpec((1024, Q_LORA), lambda j, i: (i, 0))], name="mla_in_q")[0]
    gate = project(h1n_q, w_in1, w_in1.shape[1] - Q_LORA, _ep_silu, [BF16], "mla_in_gate", col0=Q_LORA)

    wq = mla_w_q_up[0].reshape(Q_LORA, MLA_HEADS, NOPE_DIM + ROPE_DIM)
    wq_nope = wq[:, :, :NOPE_DIM].reshape(Q_LORA, MLA_HEADS // 2, 2 * NOPE_DIM)
    wq_rope = wq[:, :, NOPE_DIM:]
    wq_rot = _swap_halves(wq_rope).reshape(Q_LORA, MLA_HEADS // 2, 2 * ROPE_DIM)
    wq_rope = wq_rope.reshape(Q_LORA, MLA_HEADS // 2, 2 * ROPE_DIM)
    wq_ext = jnp.concatenate([wq_nope, wq_rope, wq_rot], axis=-1).reshape(Q_LORA, -1)
    wq_ext = wq_ext.astype(BF16)
    qf = project(cqn, wq_ext, wq_ext.shape[1], _ep_q_up, [BF16], "mla_q_up", n_tile=2 * tn, group=512,
                 extras=(cos4, sin4), extra_specs=(rope_spec(tm), rope_spec(tm)))

    attn = _attn_call(qf.reshape(b, s, -1), kv.reshape(b, s, -1), kr.reshape(b, s, -1),
                      gate.reshape(b, s, -1), tq=512)
    out = _mm_res_call(attn.reshape(m, D_INNER), mla_w_out[0].astype(BF16), h1,
                       (final_norm_g.reshape(1, d),), final_norm=True, tm=RES_TM, sub_n=RES_SUB_N,
                       name="mla_out")[0]
    return out.reshape(b, s, d)
```

```python
import functools

import numpy as np
import jax
import jax.numpy as jnp
from jax import lax
from jax.experimental import pallas as pl
from jax.experimental.pallas import tpu as pltpu

F32 = jnp.float32
BF16 = jnp.bfloat16

D_MODEL = 2048
D_INNER = 2 * D_MODEL
HG_KEY_DIM = 128
HG_HEADS = D_MODEL // HG_KEY_DIM
HG_KEY_TOTAL = HG_HEADS * HG_KEY_DIM
HG_VAL_DIM = D_INNER // HG_HEADS
HG_CHUNK = 64
HG_LEVELS = 6
MLA_HEADS = 32
Q_LORA = 768
KV_LORA = 512
NOPE_DIM = 128
ROPE_DIM = 64
V_DIM = 128
ROPE_THETA = 10000.0
EPS = 1e-6
LOG2_E = 1.4426950408889634

LANES = 128
VMEM_LIMIT = 56 * 1024 * 1024

MM_TM, MM_TN, MM_SUB_M, MM_SUB_N = 2048, 1024, 128, 512
CAST_ROWS = 256
RES_TM, RES_SUB_N = 512, 512

_NT = (((1,), (1,)), ((), ()))
_TN = (((0,), (0,)), ((), ()))


def _params(n_axes):
    return pltpu.CompilerParams(dimension_semantics=("arbitrary",) * n_axes,
                                vmem_limit_bytes=VMEM_LIMIT)


def _rms(x):
    return x * lax.rsqrt(jnp.mean(x * x, axis=-1, keepdims=True) + EPS)


def _silu(x):
    return x / (1.0 + jnp.exp(-x))


def _norm_cast_kernel(x_ref, g_ref, o_ref):
    o_ref[...] = (_rms(x_ref[...]) * g_ref[...]).astype(BF16)


def _norm_cast_call(x, g, *, tm, name):
    m, k = x.shape
    return pl.pallas_call(
        _norm_cast_kernel,
        grid=(m // tm,),
        in_specs=[pl.BlockSpec((tm, k), lambda i: (i, 0)), pl.BlockSpec((1, k), lambda i: (0, 0))],
        out_specs=pl.BlockSpec((tm, k), lambda i: (i, 0)),
        out_shape=jax.ShapeDtypeStruct((m, k), BF16),
        compiler_params=_params(1),
        name=name,
    )(x, g)


def _mm_kernel(epilogue, n_extra, sub_m, sub_n, cast_w, x_ref, w_ref, *refs):
    extras = refs[:n_extra]
    if cast_w:
        outs, wb_ref = refs[n_extra:-1], refs[-1]

        @pl.when(pl.program_id(1) == 0)
        def _():
            def cast_rows(kk, carry):
                rows = pl.ds(pl.multiple_of(kk * CAST_ROWS, CAST_ROWS), CAST_ROWS)
                wb_ref[rows, :] = w_ref[rows, :].astype(BF16)
                return carry
            lax.fori_loop(0, w_ref.shape[0] // CAST_ROWS, cast_rows, 0)
    else:
        outs, wb_ref = refs[n_extra:], w_ref
    for c0 in range(0, wb_ref.shape[1], sub_n):
        cols = slice(c0, c0 + sub_n)
        for r0 in range(0, x_ref.shape[0], sub_m):
            rows = slice(r0, r0 + sub_m)
            acc = jnp.dot(x_ref[rows, :], wb_ref[:, cols], preferred_element_type=F32)
            epilogue(acc, rows, cols, extras, outs)


def _mm_call(x, w, *, n, col0=0, tm, tn, sub_m, sub_n, epilogue, extras=(), extra_specs=(), out_shapes,
             out_specs, name):
    m, k = x.shape
    cast_w = w.dtype != BF16
    if col0 % tn == 0:
        w_spec = pl.BlockSpec((k, tn), lambda j, i: (0, j + col0 // tn))
    else:
        assert col0 % LANES == 0
        w_spec = pl.BlockSpec((pl.Element(k), pl.Element(tn)),
                              lambda j, i: (0, pl.multiple_of(col0 + j * tn, LANES)))
    kernel = functools.partial(_mm_kernel, epilogue, len(extras), sub_m, sub_n, cast_w)
    return pl.pallas_call(
        kernel,
        grid=(n // tn, m // tm),
        in_specs=[pl.BlockSpec((tm, k), lambda j, i: (i, 0)), w_spec] + list(extra_specs),
        out_specs=out_specs,
        out_shape=out_shapes,
        scratch_shapes=[pltpu.VMEM((k, tn), BF16)] if cast_w else [],
        compiler_params=_params(2),
        name=name,
    )(x, w, *extras)


def _ep_cast(acc, rows, cols, extras, outs):
    outs[0][rows, cols] = acc.astype(outs[0].dtype)


def _ep_silu(acc, rows, cols, extras, outs):
    outs[0][rows, cols] = _silu(acc).astype(outs[0].dtype)


def _ep_forget_gate(acc, rows, cols, extras, outs):
    lb_logits = extras[0][:, cols]
    mx = jnp.max(lb_logits, axis=0, keepdims=True)
    e = jnp.exp(lb_logits - mx)
    lb = e[0:1, :] / jnp.sum(e, axis=0, keepdims=True)
    t = jnp.exp(-jnp.abs(acc))
    r = 1.0 / (1.0 + t)
    pos = acc >= 0
    sig = jnp.where(pos, r, t * r)
    sig_neg = jnp.where(pos, t * r, r)
    outs[0][rows, cols] = jnp.log(lb + (1.0 - lb) * sig)
    outs[1][rows, cols] = ((1.0 - lb) * sig_neg).astype(BF16)


def _ep_rms(acc, rows, cols, extras, outs):
    outs[0][rows, :] = (_rms(acc) * extras[0][...]).astype(BF16)


def _rotary(x, table):
    y = x * table
    return y + pltpu.roll(y, ROPE_DIM, axis=1)


def _ep_kv_down(acc, rows, cols, extras, outs):
    gain, table = extras
    outs[0][rows, :] = (_rms(acc[:, :KV_LORA]) * gain[...]).astype(BF16)
    outs[1][rows, :] = _rotary(acc[:, KV_LORA:], table[rows, :]).astype(BF16)


def _ep_q_up(acc, rows, cols, extras, outs):
    (table,) = extras
    scale = (NOPE_DIM + ROPE_DIM) ** -0.5 * LOG2_E
    lane = lax.broadcasted_iota(jnp.int32, (acc.shape[0], LANES), 1)
    base = cols.start
    for h in range(2):
        keep = (lane < ROPE_DIM) if h == 0 else (lane >= ROPE_DIM)
        c0 = 256 * h
        rope = _rotary(acc[:, c0 + 128:c0 + 256], table[rows, :]) * scale
        outs[0][rows, base + c0:base + c0 + 128] = (acc[:, c0:c0 + 128] * scale).astype(BF16)
        outs[0][rows, base + c0 + 128:base + c0 + 256] = jnp.where(keep, rope, 0.0).astype(BF16)


def _mm_res_kernel(final_norm, sub_n, y_ref, w_ref, r_ref, *refs):
    if final_norm:
        g_ref, o_ref = refs
    else:
        ga_ref, gb_ref, o_ref, na_ref, nb_ref = refs
    n = w_ref.shape[1]
    ssq = jnp.zeros((y_ref.shape[0], 1), F32)
    for c0 in range(0, n, sub_n):
        cols = slice(c0, c0 + sub_n)
        h = r_ref[:, cols] + jnp.dot(y_ref[...], w_ref[:, cols], preferred_element_type=F32)
        o_ref[:, cols] = h
        ssq = ssq + jnp.sum(h * h, axis=-1, keepdims=True)
    inv = lax.rsqrt(ssq * (1.0 / n) + EPS)
    for c0 in range(0, n, sub_n):
        cols = slice(c0, c0 + sub_n)
        hn = o_ref[:, cols] * inv
        if final_norm:
            o_ref[:, cols] = hn * g_ref[:, cols]
        else:
            na_ref[:, cols] = (hn * ga_ref[:, cols]).astype(BF16)
            nb_ref[:, cols] = (hn * gb_ref[:, cols]).astype(BF16)


def _mm_res_call(y, w, res, gains, *, final_norm, tm, sub_n, name):
    m, kdim = y.shape
    n = w.shape[1]
    row_spec = pl.BlockSpec((tm, n), lambda i: (i, 0))
    gain_spec = pl.BlockSpec((1, n), lambda i: (0, 0))
    out_shape = [jax.ShapeDtypeStruct((m, n), F32)]
    out_specs = [row_spec]
    if not final_norm:
        out_shape += [jax.ShapeDtypeStruct((m, n), BF16)] * 2
        out_specs += [row_spec, row_spec]
    return pl.pallas_call(
        functools.partial(_mm_res_kernel, final_norm, sub_n),
        grid=(m // tm,),
        in_specs=[pl.BlockSpec((tm, kdim), lambda i: (i, 0)),
                  pl.BlockSpec((kdim, n), lambda i: (0, 0), pipeline_mode=pl.Buffered(1)),
                  row_spec] + [gain_spec] * len(gains),
        out_specs=out_specs,
        out_shape=out_shape,
        compiler_params=_params(1),
        name=name,
    )(y, w, res, *gains)


def _hgrn_exponent_matrix():
    c = HG_CHUNK
    t = np.arange(c)[:, None]
    u = np.arange(c)[None, :]
    mats = [(u <= t).astype(np.float32), (u > t).astype(np.float32)]
    for level in range(HG_LEVELS):
        m = 1 << level
        r = ((t >> (level + 1)) << (level + 1)) + m - 1
        upper = ((t >> level) & 1) == 1
        up = ((u > r) & (u <= t)).astype(np.float32)
        lo = ((u > t) & (u <= r)).astype(np.float32)
        mats.append(np.where(upper, up, lo))
    return np.concatenate(mats, axis=0)


def _hgrn_kernel(q_ref, lf_ref, k_ref, v_ref, sg_ref, gn_ref, e_ref, y_ref,
                 st_ref, ex_ref, qe_ref, a_ref, inc_ref, sb_ref, *, hb, ts):
    c = HG_CHUNK
    dk = HG_KEY_DIM
    dv = HG_VAL_DIM

    @pl.when(pl.program_id(2) == 0)
    def _():
        st_ref[...] = jnp.zeros_like(st_ref)

    nc = ts // c
    sub = 8
    e_mat = e_ref[...]
    t_i = lax.broadcasted_iota(jnp.int32, (c, c), 0)
    s_i = lax.broadcasted_iota(jnp.int32, (c, c), 1)
    diff = t_i ^ s_i
    causal = s_i < t_i
    pair_masks = [jnp.where(causal & ((diff >> level) == 1), 1.0, 0.0) for level in range(HG_LEVELS)]
    diag = jnp.where(t_i == s_i, 1.0, 0.0)
    row = lax.broadcasted_iota(jnp.int32, (c, dk), 0)
    uppers = [((row >> level) & 1) == 1 for level in range(HG_LEVELS)]
    gn = gn_ref[...]

    def decays(ci):
        lf = lf_ref[0, ci * c:(ci + 1) * c, :]
        hi = lf.astype(BF16)
        lo = (lf - hi.astype(F32)).astype(BF16)
        ex_ref[ci] = jnp.exp(jnp.dot(e_mat, jnp.concatenate([hi, lo], axis=0), preferred_element_type=F32))

    def chunk_local(ci):
        rows = slice(ci * c, (ci + 1) * c)
        for h in range(hb):
            idx = ci * hb + h
            cs = slice(h * dk, (h + 1) * dk)
            q = q_ref[0, rows, cs].astype(F32)
            k = k_ref[0, rows, cs].astype(F32)
            qe_ref[idx] = (q * ex_ref[ci, 0:c, cs]).astype(BF16)
            ks = (k * ex_ref[ci, c:2 * c, cs]).astype(BF16)
            inc_ref[idx] = lax.dot_general(ks, v_ref[0, rows, h * dv:(h + 1) * dv], _TN,
                                           preferred_element_type=F32)

            scores = diag * jnp.sum(q * k, axis=-1, keepdims=True)
            for level in range(HG_LEVELS):
                x = ex_ref[ci, (2 + level) * c:(3 + level) * c, cs]
                if (1 << level) >= sub:
                    w = jnp.concatenate(
                        [x[r0:r0 + sub] * (q if (r0 >> level) & 1 else k)[r0:r0 + sub] for r0 in range(0, c, sub)],
                        axis=0)
                else:
                    w = x * jnp.where(uppers[level], q, k)
                w = w.astype(BF16)
                scores = scores + lax.dot_general(w, w, _NT, preferred_element_type=F32) * pair_masks[level]
            a_ref[idx] = scores.astype(BF16)

    def recur(ci, states):
        new = []
        for h in range(hb):
            idx = ci * hb + h
            sb_ref[idx] = states[h].astype(BF16)
            decay_row = ex_ref[ci, c - 1:c, h * dk:(h + 1) * dk]
            decay_col = jnp.transpose(jnp.broadcast_to(decay_row, (dk, dk)))
            new.append(states[h] * jnp.concatenate([decay_col] * (dv // dk), axis=1) + inc_ref[idx])
        return new

    def outputs(ci):
        rows = slice(ci * c, (ci + 1) * c)
        for h in range(hb):
            idx = ci * hb + h
            vs = slice(h * dv, (h + 1) * dv)
            o = (jnp.dot(qe_ref[idx], sb_ref[idx], preferred_element_type=F32)
                 + jnp.dot(a_ref[idx], v_ref[0, rows, vs], preferred_element_type=F32))
            y = _rms(o) * gn * sg_ref[0, rows, vs].astype(F32)
            y_ref[0, rows, vs] = y.astype(BF16)

    states = [st_ref[h] for h in range(hb)]
    for it in range(nc + 2):
        if it < nc:
            decays(it)
        if 0 <= it - 1 < nc:
            chunk_local(it - 1)
            states = recur(it - 1, states)
        if 0 <= it - 2 < nc:
            outputs(it - 2)
    for h in range(hb):
        st_ref[h] = states[h]


def _hgrn_call(q, lf, k, v, sg, gn, *, hb, ts):
    b, s, _ = q.shape
    e_one = _hgrn_exponent_matrix()
    e_mat = jnp.asarray(np.concatenate([e_one, e_one], axis=1), dtype=BF16)
    kernel = functools.partial(_hgrn_kernel, hb=hb, ts=ts)
    nhc = (ts // HG_CHUNK) * hb
    key_spec = pl.BlockSpec((1, ts, hb * HG_KEY_DIM), lambda bi, hi, si: (bi, si, hi))
    val_spec = pl.BlockSpec((1, ts, hb * HG_VAL_DIM), lambda bi, hi, si: (bi, si, hi))
    return pl.pallas_call(
        kernel,
        grid=(b, HG_HEADS // hb, s // ts),
        in_specs=[key_spec, key_spec, key_spec, val_spec, val_spec,
                  pl.BlockSpec((1, HG_VAL_DIM), lambda bi, hi, si: (0, 0)),
                  pl.BlockSpec(e_mat.shape, lambda bi, hi, si: (0, 0))],
        out_specs=val_spec,
        out_shape=jax.ShapeDtypeStruct((b, s, D_INNER), BF16),
        scratch_shapes=[pltpu.VMEM((hb, HG_KEY_DIM, HG_VAL_DIM), F32),
                        pltpu.VMEM((ts // HG_CHUNK,) + (e_mat.shape[0], hb * HG_KEY_DIM), F32),
                        pltpu.VMEM((nhc, HG_CHUNK, HG_KEY_DIM), BF16),
                        pltpu.VMEM((nhc, HG_CHUNK, HG_CHUNK), BF16),
                        pltpu.VMEM((nhc, HG_KEY_DIM, HG_VAL_DIM), F32),
                        pltpu.VMEM((nhc, HG_KEY_DIM, HG_VAL_DIM), BF16)],
        compiler_params=_params(3),
        name="hgrn2_recurrence",
    )(q, lf, k, v, sg, gn, e_mat)


def _attn_kernel(q_ref, kv_ref, kr_ref, g_ref, o_ref, kcat_ref, vext_ref, *, tq):
    s_len = q_ref.shape[1]
    t_i = lax.broadcasted_iota(jnp.int32, (tq, tq), 0)
    s_i = lax.broadcasted_iota(jnp.int32, (tq, tq), 1)
    causal = s_i <= t_i

    def update(q, keys, mask, m, acc, hh):
        sc = lax.dot_general(q, kcat_ref[hh, keys, :], _NT, preferred_element_type=F32)
        if mask is not None:
            sc = jnp.where(mask, sc, -jnp.inf)
        m_new = jnp.maximum(m, jnp.max(sc, axis=-1, keepdims=True))
        p = jnp.exp2(sc - m_new)
        acc = jnp.exp2(m - m_new) * acc + jnp.dot(p.astype(BF16), vext_ref[hh, keys, :],
                                                  preferred_element_type=F32)
        return m_new, acc

    for hh in range(2):
        kcat_ref[hh, :, :NOPE_DIM] = kv_ref[0, :, hh * 256:hh * 256 + NOPE_DIM]
        kcat_ref[hh, :, NOPE_DIM:] = kr_ref[0]
        vext_ref[hh, :, :V_DIM] = kv_ref[0, :, hh * 256 + NOPE_DIM:(hh + 1) * 256]
        vext_ref[hh, :, V_DIM:] = jnp.ones((s_len, V_DIM), BF16)

        for qi in range(s_len // tq):
            rows = slice(qi * tq, (qi + 1) * tq)
            q = q_ref[0, rows, hh * 256:(hh + 1) * 256]
            m = jnp.full((tq, 1), -jnp.inf, F32)
            acc = jnp.zeros((tq, 2 * V_DIM), F32)
            for kj in range(qi + 1):
                m, acc = update(q, slice(kj * tq, (kj + 1) * tq), causal if kj == qi else None, m, acc, hh)
            out = acc[:, :V_DIM] / acc[:, V_DIM:] * g_ref[0, rows, hh * V_DIM:(hh + 1) * V_DIM].astype(F32)
            o_ref[0, rows, hh * V_DIM:(hh + 1) * V_DIM] = out.astype(BF16)


def _attn_call(q, kv, kr, gate, *, tq):
    b, s, _ = q.shape
    kernel = functools.partial(_attn_kernel, tq=tq)
    return pl.pallas_call(
        kernel,
        grid=(b, MLA_HEADS // 2),
        in_specs=[pl.BlockSpec((1, s, 512), lambda bi, hp: (bi, 0, hp)),
                  pl.BlockSpec((1, s, 512), lambda bi, hp: (bi, 0, hp)),
                  pl.BlockSpec((1, s, LANES), lambda bi, hp: (bi, 0, 0)),
                  pl.BlockSpec((1, s, 2 * V_DIM), lambda bi, hp: (bi, 0, hp))],
        out_specs=pl.BlockSpec((1, s, 2 * V_DIM), lambda bi, hp: (bi, 0, hp)),
        out_shape=jax.ShapeDtypeStruct((b, s, D_INNER), BF16),
        scratch_shapes=[pltpu.VMEM((2, s, 2 * LANES), BF16),
                        pltpu.VMEM((2, s, 2 * V_DIM), BF16)],
        compiler_params=_params(2),
        name="mla_flash_attention",
    )(q, kv, kr, gate)


def _rope_tables(seq):
    pos = np.arange(seq, dtype=np.float64)
    inv_freq = ROPE_THETA ** (-np.arange(0, ROPE_DIM, 2, dtype=np.float64) / ROPE_DIM)
    ang = pos[:, None] * inv_freq[None, :]
    cos, sin = np.cos(ang), np.sin(ang)
    return jnp.asarray(np.concatenate([cos, cos, -sin, sin], axis=-1), dtype=F32)


def _swap_halves(w):
    half = w.shape[-1] // 2
    return jnp.concatenate([w[..., half:], w[..., :half]], axis=-1)


def kernel(x, norm_g, hg_w_in, hg_g_norm, hg_w_out, hg_lb, kv_in_norm_g, w_kv_down, kv_norm_g,
           w_kv_up, mla_w_in, mla_q_norm_g, mla_w_q_up, mla_w_out, final_norm_g):
    b, s, d = x.shape
    m = b * s
    x2 = x.reshape(m, d)
    tm, tn, sub_n = MM_TM, MM_TN, MM_SUB_N
    rope_table = _rope_tables(s)

    def rope_spec(rows):
        return pl.BlockSpec((rows, LANES), lambda j, i: (i % (s // rows), 0))

    def project(xb, w, n, epilogue, out_dtypes, name, *, col0=0, rows=tm, n_tile=tn, group=sub_n,
                row_group=MM_SUB_M, extras=(), extra_specs=()):
        outs = _mm_call(xb, w, n=n, col0=col0, tm=rows, tn=n_tile, sub_m=row_group, sub_n=group,
                        epilogue=epilogue,
                        extras=extras, extra_specs=extra_specs,
                        out_shapes=[jax.ShapeDtypeStruct((m, n), dt) for dt in out_dtypes],
                        out_specs=[pl.BlockSpec((rows, n_tile), lambda j, i: (i, j)) for _ in out_dtypes],
                        name=name)
        return outs if len(outs) > 1 else outs[0]

    xn = _norm_cast_call(x2, norm_g[0].reshape(1, d), tm=1024, name="hg_norm")
    w_in = hg_w_in[0]
    c1, c2, c3 = HG_KEY_TOTAL, 2 * HG_KEY_TOTAL, 2 * HG_KEY_TOTAL + D_INNER
    q = project(xn, w_in, c1, _ep_cast, [BF16], "hg_in_q")
    lf, kg = project(xn, w_in, c2 - c1, _ep_forget_gate, [F32, BF16], "hg_in_f", col0=c1, rows=tm // 2,
                     row_group=tm // 2, extras=(hg_lb,),
                     extra_specs=(pl.BlockSpec((hg_lb.shape[0], tn), lambda j, i: (0, j)),))
    vi = project(xn, w_in, c3 - c2, _ep_cast, [BF16], "hg_in_i", col0=c2)
    sg = project(xn, w_in, w_in.shape[1] - c3, _ep_silu, [BF16], "hg_in_g", col0=c3)

    y = _hgrn_call(q.reshape(b, s, -1), lf.reshape(b, s, -1), kg.reshape(b, s, -1),
                   vi.reshape(b, s, -1), sg.reshape(b, s, -1), hg_g_norm[0].reshape(1, HG_VAL_DIM),
                   hb=2, ts=512)
    h1, h1n_kv, h1n_q = _mm_res_call(y.reshape(m, D_INNER), hg_w_out[0].astype(BF16), x2,
                                     (kv_in_norm_g.reshape(1, d), norm_g[1].reshape(1, d)),
                                     final_norm=False, tm=RES_TM, sub_n=RES_SUB_N, name="hg_out")

    w_kvd = jnp.concatenate([w_kv_down, _swap_halves(w_kv_down[:, KV_LORA:])], axis=-1)
    n_kvd = w_kvd.shape[1]
    cn, kr = _mm_call(h1n_kv, w_kvd, n=n_kvd, tm=1024, tn=n_kvd, sub_m=MM_SUB_M, sub_n=n_kvd,
                      epilogue=_ep_kv_down,
                      extras=(kv_norm_g.reshape(1, KV_LORA), rope_table),
                      extra_specs=(pl.BlockSpec((1, KV_LORA), lambda j, i: (0, 0)), rope_spec(1024)),
                      out_shapes=[jax.ShapeDtypeStruct((m, KV_LORA), BF16),
                                  jax.ShapeDtypeStruct((m, LANES), BF16)],
                      out_specs=[pl.BlockSpec((1024, KV_LORA), lambda j, i: (i, 0)),
                                 pl.BlockSpec((1024, LANES), lambda j, i: (i, 0))], name="kv_down")
    kv = project(cn, w_kv_up, w_kv_up.shape[1], _ep_cast, [BF16], "kv_up", n_tile=2 * tn)

    w_in1 = mla_w_in[0]
    cqn = _mm_call(h1n_q, w_in1, n=Q_LORA, tm=1024, tn=Q_LORA, sub_m=MM_SUB_M, sub_n=Q_LORA,
                   epilogue=_ep_rms,
                   extras=(mla_q_norm_g[0].reshape(1, Q_LORA),),
                   extra_specs=(pl.BlockSpec((1, Q_LORA), lambda j, i: (0, 0)),),
                   out_shapes=[jax.ShapeDtypeStruct((m, Q_LORA), BF16)],
                   out_specs=[pl.BlockSpec((1024, Q_LORA), lambda j, i: (i, 0))], name="mla_in_q")[0]
    gate = project(h1n_q, w_in1, w_in1.shape[1] - Q_LORA, _ep_silu, [BF16], "mla_in_gate", col0=Q_LORA)

    wq = mla_w_q_up[0].reshape(Q_LORA, MLA_HEADS, NOPE_DIM + ROPE_DIM)
    wq_ext = jnp.concatenate([wq, _swap_halves(wq[:, :, NOPE_DIM:])], axis=-1)
    wq_ext = wq_ext.reshape(Q_LORA, -1).astype(BF16)
    qf = project(cqn, wq_ext, wq_ext.shape[1], _ep_q_up, [BF16], "mla_q_up", n_tile=2 * tn, group=512,
                 extras=(rope_table,), extra_specs=(rope_spec(tm),))

    attn = _attn_call(qf.reshape(b, s, -1), kv.reshape(b, s, -1), kr.reshape(b, s, -1),
                      gate.reshape(b, s, -1), tq=512)
    out = _mm_res_call(attn.reshape(m, D_INNER), mla_w_out[0].astype(BF16), h1,
                       (final_norm_g.reshape(1, d),), final_norm=True, tm=RES_TM, sub_n=RES_SUB_N,
                       name="mla_out")[0]
    return out.reshape(b, s, d)
```

```python
import functools

import numpy as np
import jax
import jax.numpy as jnp
from jax import lax
from jax.experimental import pallas as pl
from jax.experimental.pallas import tpu as pltpu

F32 = jnp.float32
BF16 = jnp.bfloat16

D_MODEL = 2048
D_INNER = 2 * D_MODEL
HG_KEY_DIM = 128
HG_HEADS = D_MODEL // HG_KEY_DIM
HG_KEY_TOTAL = HG_HEADS * HG_KEY_DIM
HG_VAL_DIM = D_INNER // HG_HEADS
HG_CHUNK = 64
HG_LEVELS = 6
MLA_HEADS = 32
Q_LORA = 768
KV_LORA = 512
NOPE_DIM = 128
ROPE_DIM = 64
V_DIM = 128
ROPE_THETA = 10000.0
EPS = 1e-6
LOG2_E = 1.4426950408889634

LANES = 128
VMEM_LIMIT = 56 * 1024 * 1024

MM_TM, MM_TN, MM_SUB_M, MM_SUB_N = 2048, 1024, 128, 512
CAST_ROWS = 256
RES_TM, RES_SUB_N = 512, 512

_NT = (((1,), (1,)), ((), ()))
_TN = (((0,), (0,)), ((), ()))


def _params(n_axes):
    return pltpu.CompilerParams(dimension_semantics=("arbitrary",) * n_axes,
                                vmem_limit_bytes=VMEM_LIMIT)


def _rms(x):
    return x * lax.rsqrt(jnp.mean(x * x, axis=-1, keepdims=True) + EPS)


def _silu(x):
    return x / (1.0 + jnp.exp(-x))


def _norm_cast_kernel(x_ref, g_ref, o_ref):
    o_ref[...] = (_rms(x_ref[...]) * g_ref[...]).astype(BF16)


def _norm_cast_call(x, g, *, tm, name):
    m, k = x.shape
    return pl.pallas_call(
        _norm_cast_kernel,
        grid=(m // tm,),
        in_specs=[pl.BlockSpec((tm, k), lambda i: (i, 0)), pl.BlockSpec((1, k), lambda i: (0, 0))],
        out_specs=pl.BlockSpec((tm, k), lambda i: (i, 0)),
        out_shape=jax.ShapeDtypeStruct((m, k), BF16),
        compiler_params=_params(1),
        name=name,
    )(x, g)


def _prep_cast(w_ref, wb_ref, extras):
    def cast_rows(kk, carry):
        rows = pl.ds(pl.multiple_of(kk * CAST_ROWS, CAST_ROWS), CAST_ROWS)
        wb_ref[rows, :] = w_ref[rows, :].astype(BF16)
        return carry
    lax.fori_loop(0, w_ref.shape[0] // CAST_ROWS, cast_rows, 0)


def _prep_q_up(w_ref, wb_ref, extras):
    sel = extras[-1][...]
    n_in, n_out = sel.shape
    for p in range(w_ref.shape[1] // n_in):
        pair = w_ref[:, p * n_in:(p + 1) * n_in].astype(BF16)
        wb_ref[:, p * n_out:(p + 1) * n_out] = jnp.dot(pair, sel, preferred_element_type=F32).astype(BF16)


def _mm_kernel(epilogue, n_extra, sub_m, sub_n, prep, x_ref, w_ref, *refs):
    extras = refs[:n_extra]
    if prep is not None:
        outs, wb_ref = refs[n_extra:-1], refs[-1]

        @pl.when(pl.program_id(1) == 0)
        def _():
            prep(w_ref, wb_ref, extras)
    else:
        outs, wb_ref = refs[n_extra:], w_ref
    for c0 in range(0, wb_ref.shape[1], sub_n):
        cols = slice(c0, c0 + sub_n)
        for r0 in range(0, x_ref.shape[0], sub_m):
            rows = slice(r0, r0 + sub_m)
            acc = jnp.dot(x_ref[rows, :], wb_ref[:, cols], preferred_element_type=F32)
            epilogue(acc, rows, cols, extras, outs)


def _mm_call(x, w, *, n, col0=0, tm, tn, sub_m, sub_n, epilogue, extras=(), extra_specs=(), out_shapes,
             out_specs, name, prep=None, w_tn=None):
    m, k = x.shape
    if prep is None and w.dtype != BF16:
        prep = _prep_cast
    w_tn = tn if w_tn is None else w_tn
    if col0 % w_tn == 0:
        w_spec = pl.BlockSpec((k, w_tn), lambda j, i: (0, j + col0 // w_tn))
    else:
        assert col0 % LANES == 0
        w_spec = pl.BlockSpec((pl.Element(k), pl.Element(w_tn)),
                              lambda j, i: (0, pl.multiple_of(col0 + j * w_tn, LANES)))
    kernel = functools.partial(_mm_kernel, epilogue, len(extras), sub_m, sub_n, prep)
    return pl.pallas_call(
        kernel,
        grid=(n // tn, m // tm),
        in_specs=[pl.BlockSpec((tm, k), lambda j, i: (i, 0)), w_spec] + list(extra_specs),
        out_specs=out_specs,
        out_shape=out_shapes,
        scratch_shapes=[pltpu.VMEM((k, tn), BF16)] if prep is not None else [],
        compiler_params=_params(2),
        name=name,
    )(x, w, *extras)


def _ep_cast(acc, rows, cols, extras, outs):
    outs[0][rows, cols] = acc.astype(outs[0].dtype)


def _ep_silu(acc, rows, cols, extras, outs):
    outs[0][rows, cols] = _silu(acc).astype(outs[0].dtype)


def _ep_forget_gate(acc, rows, cols, extras, outs):
    lb_logits = extras[0][:, cols]
    mx = jnp.max(lb_logits, axis=0, keepdims=True)
    e = jnp.exp(lb_logits - mx)
    lb = e[0:1, :] / jnp.sum(e, axis=0, keepdims=True)
    t = jnp.exp(-jnp.abs(acc))
    r = 1.0 / (1.0 + t)
    pos = acc >= 0
    sig = jnp.where(pos, r, t * r)
    sig_neg = jnp.where(pos, t * r, r)
    outs[0][rows, cols] = jnp.log(lb + (1.0 - lb) * sig)
    outs[1][rows, cols] = ((1.0 - lb) * sig_neg).astype(BF16)


def _ep_rms(acc, rows, cols, extras, outs):
    outs[0][rows, :] = (_rms(acc) * extras[0][...]).astype(BF16)


def _rotary(x, table):
    y = x * table
    return y + pltpu.roll(y, ROPE_DIM, axis=1)


def _ep_kv_down(acc, rows, cols, extras, outs):
    gain, table = extras
    outs[0][rows, :] = (_rms(acc[:, :KV_LORA]) * gain[...]).astype(BF16)
    outs[1][rows, :] = _rotary(acc[:, KV_LORA:], table[rows, :]).astype(BF16)


def _ep_q_up(acc, rows, cols, extras, outs):
    table = extras[0]
    scale = (NOPE_DIM + ROPE_DIM) ** -0.5 * LOG2_E
    lane = lax.broadcasted_iota(jnp.int32, (acc.shape[0], LANES), 1)
    base = cols.start
    for h in range(2):
        keep = (lane < ROPE_DIM) if h == 0 else (lane >= ROPE_DIM)
        c0 = 256 * h
        rope = _rotary(acc[:, c0 + 128:c0 + 256], table[rows, :]) * scale
        outs[0][rows, base + c0:base + c0 + 128] = (acc[:, c0:c0 + 128] * scale).astype(BF16)
        outs[0][rows, base + c0 + 128:base + c0 + 256] = jnp.where(keep, rope, 0.0).astype(BF16)


def _mm_res_kernel(final_norm, sub_n, y_ref, w_ref, r_ref, *refs):
    if final_norm:
        g_ref, o_ref = refs
    else:
        ga_ref, gb_ref, o_ref, na_ref, nb_ref = refs
    n = w_ref.shape[1]
    ssq = jnp.zeros((y_ref.shape[0], 1), F32)
    for c0 in range(0, n, sub_n):
        cols = slice(c0, c0 + sub_n)
        h = r_ref[:, cols] + jnp.dot(y_ref[...], w_ref[:, cols], preferred_element_type=F32)
        o_ref[:, cols] = h
        ssq = ssq + jnp.sum(h * h, axis=-1, keepdims=True)
    inv = lax.rsqrt(ssq * (1.0 / n) + EPS)
    for c0 in range(0, n, sub_n):
        cols = slice(c0, c0 + sub_n)
        hn = o_ref[:, cols] * inv
        if final_norm:
            o_ref[:, cols] = hn * g_ref[:, cols]
        else:
            na_ref[:, cols] = (hn * ga_ref[:, cols]).astype(BF16)
            nb_ref[:, cols] = (hn * gb_ref[:, cols]).astype(BF16)


def _mm_res_call(y, w, res, gains, *, final_norm, tm, sub_n, name):
    m, kdim = y.shape
    n = w.shape[1]
    row_spec = pl.BlockSpec((tm, n), lambda i: (i, 0))
    gain_spec = pl.BlockSpec((1, n), lambda i: (0, 0))
    out_shape = [jax.ShapeDtypeStruct((m, n), F32)]
    out_specs = [row_spec]
    if not final_norm:
        out_shape += [jax.ShapeDtypeStruct((m, n), BF16)] * 2
        out_specs += [row_spec, row_spec]
    return pl.pallas_call(
        functools.partial(_mm_res_kernel, final_norm, sub_n),
        grid=(m // tm,),
        in_specs=[pl.BlockSpec((tm, kdim), lambda i: (i, 0)),
                  pl.BlockSpec((kdim, n), lambda i: (0, 0), pipeline_mode=pl.Buffered(1)),
                  row_spec] + [gain_spec] * len(gains),
        out_specs=out_specs,
        out_shape=out_shape,
        compiler_params=_params(1),
        name=name,
    )(y, w, res, *gains)


def _hgrn_exponent_matrix():
    c = HG_CHUNK
    t = np.arange(c)[:, None]
    u = np.arange(c)[None, :]
    mats = [(u <= t).astype(np.float32), (u > t).astype(np.float32)]
    for level in range(HG_LEVELS):
        m = 1 << level
        r = ((t >> (level + 1)) << (level + 1)) + m - 1
        upper = ((t >> level) & 1) == 1
        up = ((u > r) & (u <= t)).astype(np.float32)
        lo = ((u > t) & (u <= r)).astype(np.float32)
        mats.append(np.where(upper, up, lo))
    return np.concatenate(mats, axis=0)


def _hgrn_kernel(q_ref, lf_ref, k_ref, v_ref, sg_ref, gn_ref, e_ref, y_ref,
                 st_ref, ex_ref, qe_ref, a_ref, inc_ref, sb_ref, *, hb, ts):
    c = HG_CHUNK
    dk = HG_KEY_DIM
    dv = HG_VAL_DIM

    @pl.when(pl.program_id(2) == 0)
    def _():
        st_ref[...] = jnp.zeros_like(st_ref)

    nc = ts // c
    sub = 8
    e_mat = e_ref[...]
    t_i = lax.broadcasted_iota(jnp.int32, (c, c), 0)
    s_i = lax.broadcasted_iota(jnp.int32, (c, c), 1)
    diff = t_i ^ s_i
    causal = s_i < t_i
    pair_masks = [jnp.where(causal & ((diff >> level) == 1), 1.0, 0.0) for level in range(HG_LEVELS)]
    diag = jnp.where(t_i == s_i, 1.0, 0.0)
    row = lax.broadcasted_iota(jnp.int32, (c, dk), 0)
    uppers = [((row >> level) & 1) == 1 for level in range(HG_LEVELS)]
    gn = gn_ref[...]

    def decays(ci):
        lf = lf_ref[0, ci * c:(ci + 1) * c, :]
        hi = lf.astype(BF16)
        lo = (lf - hi.astype(F32)).astype(BF16)
        ex_ref[ci] = jnp.exp(jnp.dot(e_mat, jnp.concatenate([hi, lo], axis=0), preferred_element_type=F32))

    def chunk_local(ci):
        rows = slice(ci * c, (ci + 1) * c)
        for h in range(hb):
            idx = ci * hb + h
            cs = slice(h * dk, (h + 1) * dk)
            q = q_ref[0, rows, cs].astype(F32)
            k = k_ref[0, rows, cs].astype(F32)
            qe_ref[idx] = (q * ex_ref[ci, 0:c, cs]).astype(BF16)
            ks = (k * ex_ref[ci, c:2 * c, cs]).astype(BF16)
            inc_ref[idx] = lax.dot_general(ks, v_ref[0, rows, h * dv:(h + 1) * dv], _TN,
                                           preferred_element_type=F32)

            scores = diag * jnp.sum(q * k, axis=-1, keepdims=True)
            for level in range(HG_LEVELS):
                x = ex_ref[ci, (2 + level) * c:(3 + level) * c, cs]
                if (1 << level) >= sub:
                    w = jnp.concatenate(
                        [x[r0:r0 + sub] * (q if (r0 >> level) & 1 else k)[r0:r0 + sub] for r0 in range(0, c, sub)],
                        axis=0)
                else:
                    w = x * jnp.where(uppers[level], q, k)
                w = w.astype(BF16)
                scores = scores + lax.dot_general(w, w, _NT, preferred_element_type=F32) * pair_masks[level]
            a_ref[idx] = scores.astype(BF16)

    def recur(ci, states):
        new = []
        for h in range(hb):
            idx = ci * hb + h
            sb_ref[idx] = states[h].astype(BF16)
            decay_row = ex_ref[ci, c - 1:c, h * dk:(h + 1) * dk]
            decay_col = jnp.transpose(jnp.broadcast_to(decay_row, (dk, dk)))
            new.append(states[h] * jnp.concatenate([decay_col] * (dv // dk), axis=1) + inc_ref[idx])
        return new

    def outputs(ci):
        rows = slice(ci * c, (ci + 1) * c)
        for h in range(hb):
            idx = ci * hb + h
            vs = slice(h * dv, (h + 1) * dv)
            o = (jnp.dot(qe_ref[idx], sb_ref[idx], preferred_element_type=F32)
                 + jnp.dot(a_ref[idx], v_ref[0, rows, vs], preferred_element_type=F32))
            y = _rms(o) * gn * sg_ref[0, rows, vs].astype(F32)
            y_ref[0, rows, vs] = y.astype(BF16)

    states = [st_ref[h] for h in range(hb)]
    for it in range(nc + 2):
        if it < nc:
            decays(it)
        if 0 <= it - 1 < nc:
            chunk_local(it - 1)
            states = recur(it - 1, states)
        if 0 <= it - 2 < nc:
            outputs(it - 2)
    for h in range(hb):
        st_ref[h] = states[h]


def _hgrn_call(q, lf, k, v, sg, gn, *, hb, ts):
    b, s, _ = q.shape
    e_one = _hgrn_exponent_matrix()
    e_mat = jnp.asarray(np.concatenate([e_one, e_one], axis=1), dtype=BF16)
    kernel = functools.partial(_hgrn_kernel, hb=hb, ts=ts)
    nhc = (ts // HG_CHUNK) * hb
    key_spec = pl.BlockSpec((1, ts, hb * HG_KEY_DIM), lambda bi, hi, si: (bi, si, hi))
    val_spec = pl.BlockSpec((1, ts, hb * HG_VAL_DIM), lambda bi, hi, si: (bi, si, hi))
    return pl.pallas_call(
        kernel,
        grid=(b, HG_HEADS // hb, s // ts),
        in_specs=[key_spec, key_spec, key_spec, val_spec, val_spec,
                  pl.BlockSpec((1, HG_VAL_DIM), lambda bi, hi, si: (0, 0)),
                  pl.BlockSpec(e_mat.shape, lambda bi, hi, si: (0, 0))],
        out_specs=val_spec,
        out_shape=jax.ShapeDtypeStruct((b, s, D_INNER), BF16),
        scratch_shapes=[pltpu.VMEM((hb, HG_KEY_DIM, HG_VAL_DIM), F32),
                        pltpu.VMEM((ts // HG_CHUNK,) + (e_mat.shape[0], hb * HG_KEY_DIM), F32),
                        pltpu.VMEM((nhc, HG_CHUNK, HG_KEY_DIM), BF16),
                        pltpu.VMEM((nhc, HG_CHUNK, HG_CHUNK), BF16),
                        pltpu.VMEM((nhc, HG_KEY_DIM, HG_VAL_DIM), F32),
                        pltpu.VMEM((nhc, HG_KEY_DIM, HG_VAL_DIM), BF16)],
        compiler_params=_params(3),
        name="hgrn2_recurrence",
    )(q, lf, k, v, sg, gn, e_mat)


def _attn_kernel(q_ref, kv_ref, kr_ref, g_ref, o_ref, kcat_ref, vext_ref, *, tq):
    s_len = q_ref.shape[1]
    t_i = lax.broadcasted_iota(jnp.int32, (tq, tq), 0)
    s_i = lax.broadcasted_iota(jnp.int32, (tq, tq), 1)
    causal = s_i <= t_i

    def update(q, keys, mask, m, acc, hh):
        sc = lax.dot_general(q, kcat_ref[hh, keys, :], _NT, preferred_element_type=F32)
        if mask is not None:
            sc = jnp.where(mask, sc, -jnp.inf)
        m_new = jnp.maximum(m, jnp.max(sc, axis=-1, keepdims=True))
        p = jnp.exp2(sc - m_new)
        acc = jnp.exp2(m - m_new) * acc + jnp.dot(p.astype(BF16), vext_ref[hh, keys, :],
                                                  preferred_element_type=F32)
        return m_new, acc

    for hh in range(2):
        kcat_ref[hh, :, :NOPE_DIM] = kv_ref[0, :, hh * 256:hh * 256 + NOPE_DIM]
        kcat_ref[hh, :, NOPE_DIM:] = kr_ref[0]
        vext_ref[hh, :, :V_DIM] = kv_ref[0, :, hh * 256 + NOPE_DIM:(hh + 1) * 256]
        vext_ref[hh, :, V_DIM:] = jnp.ones((s_len, V_DIM), BF16)

        for qi in range(s_len // tq):
            rows = slice(qi * tq, (qi + 1) * tq)
            q = q_ref[0, rows, hh * 256:(hh + 1) * 256]
            m = jnp.full((tq, 1), -jnp.inf, F32)
            acc = jnp.zeros((tq, 2 * V_DIM), F32)
            for kj in range(qi + 1):
                m, acc = update(q, slice(kj * tq, (kj + 1) * tq), causal if kj == qi else None, m, acc, hh)
            out = acc[:, :V_DIM] / acc[:, V_DIM:] * g_ref[0, rows, hh * V_DIM:(hh + 1) * V_DIM].astype(F32)
            o_ref[0, rows, hh * V_DIM:(hh + 1) * V_DIM] = out.astype(BF16)


def _attn_call(q, kv, kr, gate, *, tq):
    b, s, _ = q.shape
    kernel = functools.partial(_attn_kernel, tq=tq)
    return pl.pallas_call(
        kernel,
        grid=(b, MLA_HEADS // 2),
        in_specs=[pl.BlockSpec((1, s, 512), lambda bi, hp: (bi, 0, hp)),
                  pl.BlockSpec((1, s, 512), lambda bi, hp: (bi, 0, hp)),
                  pl.BlockSpec((1, s, LANES), lambda bi, hp: (bi, 0, 0)),
                  pl.BlockSpec((1, s, 2 * V_DIM), lambda bi, hp: (bi, 0, hp))],
        out_specs=pl.BlockSpec((1, s, 2 * V_DIM), lambda bi, hp: (bi, 0, hp)),
        out_shape=jax.ShapeDtypeStruct((b, s, D_INNER), BF16),
        scratch_shapes=[pltpu.VMEM((2, s, 2 * LANES), BF16),
                        pltpu.VMEM((2, s, 2 * V_DIM), BF16)],
        compiler_params=_params(2),
        name="mla_flash_attention",
    )(q, kv, kr, gate)


def _rope_tables(seq):
    pos = np.arange(seq, dtype=np.float64)
    inv_freq = ROPE_THETA ** (-np.arange(0, ROPE_DIM, 2, dtype=np.float64) / ROPE_DIM)
    ang = pos[:, None] * inv_freq[None, :]
    cos, sin = np.cos(ang), np.sin(ang)
    return jnp.asarray(np.concatenate([cos, cos, -sin, sin], axis=-1), dtype=F32)


def _q_pair_selection():
    d_in, d_out = NOPE_DIM + ROPE_DIM, 2 * LANES
    sel = np.zeros((2 * d_in, 2 * d_out), np.float32)
    half = ROPE_DIM // 2
    for h in range(2):
        for c in range(d_in):
            sel[h * d_in + c, h * d_out + c] = 1.0
        for c in range(ROPE_DIM):
            sel[h * d_in + NOPE_DIM + (c + half) % ROPE_DIM, h * d_out + d_in + c] = 1.0
    return sel


def _swap_halves(w):
    half = w.shape[-1] // 2
    return jnp.concatenate([w[..., half:], w[..., :half]], axis=-1)


def kernel(x, norm_g, hg_w_in, hg_g_norm, hg_w_out, hg_lb, kv_in_norm_g, w_kv_down, kv_norm_g,
           w_kv_up, mla_w_in, mla_q_norm_g, mla_w_q_up, mla_w_out, final_norm_g):
    b, s, d = x.shape
    m = b * s
    x2 = x.reshape(m, d)
    tm, tn, sub_n = MM_TM, MM_TN, MM_SUB_N
    rope_table = _rope_tables(s)

    def rope_spec(rows):
        return pl.BlockSpec((rows, LANES), lambda j, i: (i % (s // rows), 0))

    def project(xb, w, n, epilogue, out_dtypes, name, *, col0=0, rows=tm, n_tile=tn, group=sub_n,
                row_group=MM_SUB_M, extras=(), extra_specs=()):
        outs = _mm_call(xb, w, n=n, col0=col0, tm=rows, tn=n_tile, sub_m=row_group, sub_n=group,
                        epilogue=epilogue,
                        extras=extras, extra_specs=extra_specs,
                        out_shapes=[jax.ShapeDtypeStruct((m, n), dt) for dt in out_dtypes],
                        out_specs=[pl.BlockSpec((rows, n_tile), lambda j, i: (i, j)) for _ in out_dtypes],
                        name=name)
        return outs if len(outs) > 1 else outs[0]

    xn = _norm_cast_call(x2, norm_g[0].reshape(1, d), tm=1024, name="hg_norm")
    w_in = hg_w_in[0]
    c1, c2, c3 = HG_KEY_TOTAL, 2 * HG_KEY_TOTAL, 2 * HG_KEY_TOTAL + D_INNER
    q = project(xn, w_in, c1, _ep_cast, [BF16], "hg_in_q")
    lf, kg = project(xn, w_in, c2 - c1, _ep_forget_gate, [F32, BF16], "hg_in_f", col0=c1, rows=tm // 2,
                     row_group=tm // 2, extras=(hg_lb,),
                     extra_specs=(pl.BlockSpec((hg_lb.shape[0], tn), lambda j, i: (0, j)),))
    vi = project(xn, w_in, c3 - c2, _ep_cast, [BF16], "hg_in_i", col0=c2)
    sg = project(xn, w_in, w_in.shape[1] - c3, _ep_silu, [BF16], "hg_in_g", col0=c3)

    y = _hgrn_call(q.reshape(b, s, -1), lf.reshape(b, s, -1), kg.reshape(b, s, -1),
                   vi.reshape(b, s, -1), sg.reshape(b, s, -1), hg_g_norm[0].reshape(1, HG_VAL_DIM),
                   hb=4, ts=512)
    h1, h1n_kv, h1n_q = _mm_res_call(y.reshape(m, D_INNER), hg_w_out[0].astype(BF16), x2,
                                     (kv_in_norm_g.reshape(1, d), norm_g[1].reshape(1, d)),
                                     final_norm=False, tm=RES_TM, sub_n=RES_SUB_N, name="hg_out")

    w_kvd = jnp.concatenate([w_kv_down, _swap_halves(w_kv_down[:, KV_LORA:])], axis=-1)
    n_kvd = w_kvd.shape[1]
    cn, kr = _mm_call(h1n_kv, w_kvd, n=n_kvd, tm=1024, tn=n_kvd, sub_m=MM_SUB_M, sub_n=n_kvd,
                      epilogue=_ep_kv_down,
                      extras=(kv_norm_g.reshape(1, KV_LORA), rope_table),
                      extra_specs=(pl.BlockSpec((1, KV_LORA), lambda j, i: (0, 0)), rope_spec(1024)),
                      out_shapes=[jax.ShapeDtypeStruct((m, KV_LORA), BF16),
                                  jax.ShapeDtypeStruct((m, LANES), BF16)],
                      out_specs=[pl.BlockSpec((1024, KV_LORA), lambda j, i: (i, 0)),
                                 pl.BlockSpec((1024, LANES), lambda j, i: (i, 0))], name="kv_down")
    kv = project(cn, w_kv_up, w_kv_up.shape[1], _ep_cast, [BF16], "kv_up", n_tile=2 * tn)

    w_in1 = mla_w_in[0]
    cqn = _mm_call(h1n_q, w_in1, n=Q_LORA, tm=1024, tn=Q_LORA, sub_m=MM_SUB_M, sub_n=Q_LORA,
                   epilogue=_ep_rms,
                   extras=(mla_q_norm_g[0].reshape(1, Q_LORA),),
                   extra_specs=(pl.BlockSpec((1, Q_LORA), lambda j, i: (0, 0)),),
                   out_shapes=[jax.ShapeDtypeStruct((m, Q_LORA), BF16)],
                   out_specs=[pl.BlockSpec((1024, Q_LORA), lambda j, i: (i, 0))], name="mla_in_q")[0]
    gate = project(h1n_q, w_in1, w_in1.shape[1] - Q_LORA, _ep_silu, [BF16], "mla_in_gate", col0=Q_LORA)

    sel = jnp.asarray(_q_pair_selection(), dtype=BF16)
    heads_per_tile = 2 * tn // (2 * LANES)
    n_q = MLA_HEADS * 2 * LANES
    qf = _mm_call(cqn, mla_w_q_up[0], n=n_q, tm=tm, tn=2 * tn, w_tn=heads_per_tile * (NOPE_DIM + ROPE_DIM),
                  sub_m=MM_SUB_M, sub_n=512, epilogue=_ep_q_up, prep=_prep_q_up,
                  extras=(rope_table, sel),
                  extra_specs=(rope_spec(tm), pl.BlockSpec(sel.shape, lambda j, i: (0, 0))),
                  out_shapes=[jax.ShapeDtypeStruct((m, n_q), BF16)],
                  out_specs=[pl.BlockSpec((tm, 2 * tn), lambda j, i: (i, j))], name="mla_q_up")[0]

    attn = _attn_call(qf.reshape(b, s, -1), kv.reshape(b, s, -1), kr.reshape(b, s, -1),
                      gate.reshape(b, s, -1), tq=512)
    out = _mm_res_call(attn.reshape(m, D_INNER), mla_w_out[0].astype(BF16), h1,
                       (final_norm_g.reshape(1, d),), final_norm=True, tm=RES_TM, sub_n=RES_SUB_N,
                       name="mla_out")[0]
    return out.reshape(b, s, d)
```

```python
import functools

import numpy as np
import jax
import jax.numpy as jnp
from jax import lax
from jax.experimental import pallas as pl
from jax.experimental.pallas import tpu as pltpu

F32 = jnp.float32
BF16 = jnp.bfloat16

D_MODEL = 2048
D_INNER = 2 * D_MODEL
HG_KEY_DIM = 128
HG_HEADS = D_MODEL // HG_KEY_DIM
HG_KEY_TOTAL = HG_HEADS * HG_KEY_DIM
HG_VAL_DIM = D_INNER // HG_HEADS
HG_CHUNK = 64
HG_LEVELS = 6
MLA_HEADS = 32
Q_LORA = 768
KV_LORA = 512
NOPE_DIM = 128
ROPE_DIM = 64
V_DIM = 128
ROPE_THETA = 10000.0
EPS = 1e-6
LOG2_E = 1.4426950408889634

LANES = 128
VMEM_LIMIT = 56 * 1024 * 1024

MM_TM, MM_TN, MM_SUB_M, MM_SUB_N = 2048, 1024, 128, 512
CAST_ROWS = 256
RES_TM, RES_SUB_N = 512, 512

_NT = (((1,), (1,)), ((), ()))
_TN = (((0,), (0,)), ((), ()))


def _params(n_axes):
    return pltpu.CompilerParams(dimension_semantics=("arbitrary",) * n_axes,
                                vmem_limit_bytes=VMEM_LIMIT)


def _rms(x):
    return x * lax.rsqrt(jnp.mean(x * x, axis=-1, keepdims=True) + EPS)


def _silu(x):
    return x / (1.0 + jnp.exp(-x))


def _norm_cast_kernel(x_ref, g_ref, o_ref):
    o_ref[...] = (_rms(x_ref[...]) * g_ref[...]).astype(BF16)


def _norm_cast_call(x, g, *, tm, name):
    m, k = x.shape
    return pl.pallas_call(
        _norm_cast_kernel,
        grid=(m // tm,),
        in_specs=[pl.BlockSpec((tm, k), lambda i: (i, 0)), pl.BlockSpec((1, k), lambda i: (0, 0))],
        out_specs=pl.BlockSpec((tm, k), lambda i: (i, 0)),
        out_shape=jax.ShapeDtypeStruct((m, k), BF16),
        compiler_params=_params(1),
        name=name,
    )(x, g)


def _prep_cast(w_ref, wb_ref, extras):
    def cast_rows(kk, carry):
        rows = pl.ds(pl.multiple_of(kk * CAST_ROWS, CAST_ROWS), CAST_ROWS)
        wb_ref[rows, :] = w_ref[rows, :].astype(BF16)
        return carry
    lax.fori_loop(0, w_ref.shape[0] // CAST_ROWS, cast_rows, 0)


def _prep_q_up(w_ref, wb_ref, extras):
    sel = extras[-1][...]
    n_in, n_out = sel.shape
    for p in range(w_ref.shape[1] // n_in):
        pair = w_ref[:, p * n_in:(p + 1) * n_in].astype(BF16)
        wb_ref[:, p * n_out:(p + 1) * n_out] = jnp.dot(pair, sel, preferred_element_type=F32).astype(BF16)


def _mm_kernel(epilogue, n_extra, sub_m, sub_n, prep, x_ref, w_ref, *refs):
    extras = refs[:n_extra]
    if prep is not None:
        outs, wb_ref = refs[n_extra:-1], refs[-1]

        @pl.when(pl.program_id(1) == 0)
        def _():
            prep(w_ref, wb_ref, extras)
    else:
        outs, wb_ref = refs[n_extra:], w_ref

    def tile(ep):
        for c0 in range(0, wb_ref.shape[1], sub_n):
            cols = slice(c0, c0 + sub_n)
            for r0 in range(0, x_ref.shape[0], sub_m):
                rows = slice(r0, r0 + sub_m)
                acc = jnp.dot(x_ref[rows, :], wb_ref[:, cols], preferred_element_type=F32)
                ep(acc, rows, cols, extras, outs)

    if callable(epilogue):
        tile(epilogue)
    else:
        j = pl.program_id(0)
        firsts = [first for first, _ in epilogue] + [pl.num_programs(0)]
        for v, (first, ep) in enumerate(epilogue):
            pl.when((j >= first) & (j < firsts[v + 1]))(functools.partial(tile, ep))


def _mm_call(x, w, *, n, col0=0, tm, tn, sub_m, sub_n, epilogue, extras=(), extra_specs=(), out_shapes,
             out_specs, name, prep=None, w_tn=None, w_tile_of=None):
    m, k = x.shape
    if prep is None and w.dtype != BF16:
        prep = _prep_cast
    w_tn = tn if w_tn is None else w_tn
    if w_tile_of is not None:
        w_spec = pl.BlockSpec((k, w_tn), lambda j, i: (0, w_tile_of(j)))
    elif col0 % w_tn == 0:
        w_spec = pl.BlockSpec((k, w_tn), lambda j, i: (0, j + col0 // w_tn))
    else:
        assert col0 % LANES == 0
        w_spec = pl.BlockSpec((pl.Element(k), pl.Element(w_tn)),
                              lambda j, i: (0, pl.multiple_of(col0 + j * w_tn, LANES)))
    kernel = functools.partial(_mm_kernel, epilogue, len(extras), sub_m, sub_n, prep)
    return pl.pallas_call(
        kernel,
        grid=(n // tn, m // tm),
        in_specs=[pl.BlockSpec((tm, k), lambda j, i: (i, 0)), w_spec] + list(extra_specs),
        out_specs=out_specs,
        out_shape=out_shapes,
        scratch_shapes=[pltpu.VMEM((k, tn), BF16)] if prep is not None else [],
        compiler_params=_params(2),
        name=name,
    )(x, w, *extras)


def _ep_cast(acc, rows, cols, extras, outs):
    outs[0][rows, cols] = acc.astype(outs[0].dtype)


def _ep_silu(acc, rows, cols, extras, outs):
    outs[0][rows, cols] = _silu(acc).astype(outs[0].dtype)


def _ep_forget_gate(acc, rows, cols, extras, outs):
    lb_logits = extras[0][:, cols]
    mx = jnp.max(lb_logits, axis=0, keepdims=True)
    e = jnp.exp(lb_logits - mx)
    lb = e[0:1, :] / jnp.sum(e, axis=0, keepdims=True)
    t = jnp.exp(-jnp.abs(acc))
    r = 1.0 / (1.0 + t)
    pos = acc >= 0
    sig = jnp.where(pos, r, t * r)
    sig_neg = jnp.where(pos, t * r, r)
    outs[0][rows, cols] = jnp.log(lb + (1.0 - lb) * sig)
    outs[1][rows, cols] = ((1.0 - lb) * sig_neg).astype(BF16)


def _ep_rms(acc, rows, cols, extras, outs):
    outs[0][rows, :] = (_rms(acc) * extras[0][...]).astype(BF16)


def _rotary(x, table):
    y = x * table
    return y + pltpu.roll(y, ROPE_DIM, axis=1)


def _ep_kv_down(acc, rows, cols, extras, outs):
    gain, table = extras
    outs[0][rows, :] = (_rms(acc[:, :KV_LORA]) * gain[...]).astype(BF16)
    outs[1][rows, :] = _rotary(acc[:, KV_LORA:], table[rows, :]).astype(BF16)


def _ep_q_up(acc, rows, cols, extras, outs):
    table = extras[0]
    scale = (NOPE_DIM + ROPE_DIM) ** -0.5 * LOG2_E
    lane = lax.broadcasted_iota(jnp.int32, (acc.shape[0], LANES), 1)
    base = cols.start
    for h in range(2):
        keep = (lane < ROPE_DIM) if h == 0 else (lane >= ROPE_DIM)
        c0 = 256 * h
        rope = _rotary(acc[:, c0 + 128:c0 + 256], table[rows, :]) * scale
        outs[0][rows, base + c0:base + c0 + 128] = (acc[:, c0:c0 + 128] * scale).astype(BF16)
        outs[0][rows, base + c0 + 128:base + c0 + 256] = jnp.where(keep, rope, 0.0).astype(BF16)


def _mm_res_kernel(final_norm, sub_n, y_ref, w_ref, r_ref, *refs):
    if final_norm:
        g_ref, o_ref = refs
    else:
        ga_ref, gb_ref, o_ref, na_ref, nb_ref = refs
    n = w_ref.shape[1]
    ssq = jnp.zeros((y_ref.shape[0], 1), F32)
    for c0 in range(0, n, sub_n):
        cols = slice(c0, c0 + sub_n)
        h = r_ref[:, cols] + jnp.dot(y_ref[...], w_ref[:, cols], preferred_element_type=F32)
        o_ref[:, cols] = h
        ssq = ssq + jnp.sum(h * h, axis=-1, keepdims=True)
    inv = lax.rsqrt(ssq * (1.0 / n) + EPS)
    for c0 in range(0, n, sub_n):
        cols = slice(c0, c0 + sub_n)
        hn = o_ref[:, cols] * inv
        if final_norm:
            o_ref[:, cols] = hn * g_ref[:, cols]
        else:
            na_ref[:, cols] = (hn * ga_ref[:, cols]).astype(BF16)
            nb_ref[:, cols] = (hn * gb_ref[:, cols]).astype(BF16)


def _mm_res_call(y, w, res, gains, *, final_norm, tm, sub_n, name):
    m, kdim = y.shape
    n = w.shape[1]
    row_spec = pl.BlockSpec((tm, n), lambda i: (i, 0))
    gain_spec = pl.BlockSpec((1, n), lambda i: (0, 0))
    out_shape = [jax.ShapeDtypeStruct((m, n), F32)]
    out_specs = [row_spec]
    if not final_norm:
        out_shape += [jax.ShapeDtypeStruct((m, n), BF16)] * 2
        out_specs += [row_spec, row_spec]
    return pl.pallas_call(
        functools.partial(_mm_res_kernel, final_norm, sub_n),
        grid=(m // tm,),
        in_specs=[pl.BlockSpec((tm, kdim), lambda i: (i, 0)),
                  pl.BlockSpec((kdim, n), lambda i: (0, 0), pipeline_mode=pl.Buffered(1)),
                  row_spec] + [gain_spec] * len(gains),
        out_specs=out_specs,
        out_shape=out_shape,
        compiler_params=_params(1),
        name=name,
    )(y, w, res, *gains)


def _hgrn_exponent_matrix():
    c = HG_CHUNK
    t = np.arange(c)[:, None]
    u = np.arange(c)[None, :]
    mats = [(u <= t).astype(np.float32), (u > t).astype(np.float32)]
    for level in range(HG_LEVELS):
        m = 1 << level
        r = ((t >> (level + 1)) << (level + 1)) + m - 1
        upper = ((t >> level) & 1) == 1
        up = ((u > r) & (u <= t)).astype(np.float32)
        lo = ((u > t) & (u <= r)).astype(np.float32)
        mats.append(np.where(upper, up, lo))
    return np.concatenate(mats, axis=0)


def _hgrn_kernel(q_ref, lf_ref, k_ref, v_ref, sg_ref, gn_ref, e_ref, y_ref,
                 st_ref, ex_ref, qe_ref, a_ref, inc_ref, sb_ref, *, hb, ts):
    c = HG_CHUNK
    dk = HG_KEY_DIM
    dv = HG_VAL_DIM

    @pl.when(pl.program_id(2) == 0)
    def _():
        st_ref[...] = jnp.zeros_like(st_ref)

    nc = ts // c
    sub = 8
    e_mat = e_ref[...]
    t_i = lax.broadcasted_iota(jnp.int32, (c, c), 0)
    s_i = lax.broadcasted_iota(jnp.int32, (c, c), 1)
    diff = t_i ^ s_i
    causal = s_i < t_i
    pair_masks = [jnp.where(causal & ((diff >> level) == 1), 1.0, 0.0) for level in range(HG_LEVELS)]
    diag = jnp.where(t_i == s_i, 1.0, 0.0)
    row = lax.broadcasted_iota(jnp.int32, (c, dk), 0)
    uppers = [((row >> level) & 1) == 1 for level in range(HG_LEVELS)]
    gn = gn_ref[...]

    def decays(ci):
        lf = lf_ref[0, ci * c:(ci + 1) * c, :]
        hi = lf.astype(BF16)
        lo = (lf - hi.astype(F32)).astype(BF16)
        ex_ref[ci] = jnp.exp(jnp.dot(e_mat, jnp.concatenate([hi, lo], axis=0), preferred_element_type=F32))

    def chunk_local(ci):
        rows = slice(ci * c, (ci + 1) * c)
        for h in range(hb):
            idx = ci * hb + h
            cs = slice(h * dk, (h + 1) * dk)
            q = q_ref[0, rows, cs].astype(F32)
            k = k_ref[0, rows, cs].astype(F32)
            qe_ref[idx] = (q * ex_ref[ci, 0:c, cs]).astype(BF16)
            ks = (k * ex_ref[ci, c:2 * c, cs]).astype(BF16)
            inc_ref[idx] = lax.dot_general(ks, v_ref[0, rows, h * dv:(h + 1) * dv], _TN,
                                           preferred_element_type=F32)

            scores = diag * jnp.sum(q * k, axis=-1, keepdims=True)
            for level in range(HG_LEVELS):
                x = ex_ref[ci, (2 + level) * c:(3 + level) * c, cs]
                if (1 << level) >= sub:
                    w = jnp.concatenate(
                        [x[r0:r0 + sub] * (q if (r0 >> level) & 1 else k)[r0:r0 + sub] for r0 in range(0, c, sub)],
                        axis=0)
                else:
                    w = x * jnp.where(uppers[level], q, k)
                w = w.astype(BF16)
                scores = scores + lax.dot_general(w, w, _NT, preferred_element_type=F32) * pair_masks[level]
            a_ref[idx] = scores.astype(BF16)

    def recur(ci, states):
        new = []
        for h in range(hb):
            idx = ci * hb + h
            sb_ref[idx] = states[h].astype(BF16)
            decay_row = ex_ref[ci, c - 1:c, h * dk:(h + 1) * dk]
            decay_col = jnp.transpose(jnp.broadcast_to(decay_row, (dk, dk)))
            new.append(states[h] * jnp.concatenate([decay_col] * (dv // dk), axis=1) + inc_ref[idx])
        return new

    def outputs(ci):
        rows = slice(ci * c, (ci + 1) * c)
        for h in range(hb):
            idx = ci * hb + h
            vs = slice(h * dv, (h + 1) * dv)
            o = (jnp.dot(qe_ref[idx], sb_ref[idx], preferred_element_type=F32)
                 + jnp.dot(a_ref[idx], v_ref[0, rows, vs], preferred_element_type=F32))
            y = _rms(o) * gn * sg_ref[0, rows, vs].astype(F32)
            y_ref[0, rows, vs] = y.astype(BF16)

    states = [st_ref[h] for h in range(hb)]
    for it in range(nc + 2):
        if it < nc:
            decays(it)
        if 0 <= it - 1 < nc:
            chunk_local(it - 1)
            states = recur(it - 1, states)
        if 0 <= it - 2 < nc:
            outputs(it - 2)
    for h in range(hb):
        st_ref[h] = states[h]


def _hgrn_call(q, lf, k, v, sg, gn, *, hb, ts, col_offsets=(0, 0, 0)):
    b, s, _ = q.shape
    e_one = _hgrn_exponent_matrix()
    e_mat = jnp.asarray(np.concatenate([e_one, e_one], axis=1), dtype=BF16)
    kernel = functools.partial(_hgrn_kernel, hb=hb, ts=ts)
    nhc = (ts // HG_CHUNK) * hb

    def spec(width, col0=0):
        return pl.BlockSpec((1, ts, width), lambda bi, hi, si: (bi, si, hi + col0 // width))

    key_w, val_w = hb * HG_KEY_DIM, hb * HG_VAL_DIM
    key_spec, val_spec = spec(key_w), spec(val_w)
    return pl.pallas_call(
        kernel,
        grid=(b, HG_HEADS // hb, s // ts),
        in_specs=[spec(key_w, col_offsets[0]), key_spec, key_spec, spec(val_w, col_offsets[1]),
                  spec(val_w, col_offsets[2]),
                  pl.BlockSpec((1, HG_VAL_DIM), lambda bi, hi, si: (0, 0)),
                  pl.BlockSpec(e_mat.shape, lambda bi, hi, si: (0, 0))],
        out_specs=val_spec,
        out_shape=jax.ShapeDtypeStruct((b, s, D_INNER), BF16),
        scratch_shapes=[pltpu.VMEM((hb, HG_KEY_DIM, HG_VAL_DIM), F32),
                        pltpu.VMEM((ts // HG_CHUNK,) + (e_mat.shape[0], hb * HG_KEY_DIM), F32),
                        pltpu.VMEM((nhc, HG_CHUNK, HG_KEY_DIM), BF16),
                        pltpu.VMEM((nhc, HG_CHUNK, HG_CHUNK), BF16),
                        pltpu.VMEM((nhc, HG_KEY_DIM, HG_VAL_DIM), F32),
                        pltpu.VMEM((nhc, HG_KEY_DIM, HG_VAL_DIM), BF16)],
        compiler_params=_params(3),
        name="hgrn2_recurrence",
    )(q, lf, k, v, sg, gn, e_mat)


def _attn_kernel(q_ref, kv_ref, kr_ref, g_ref, o_ref, *, tq, heads):
    s_len = q_ref.shape[1]
    t_i = lax.broadcasted_iota(jnp.int32, (tq, tq), 0)
    s_i = lax.broadcasted_iota(jnp.int32, (tq, tq), 1)
    causal = s_i <= t_i
    ones = jnp.ones((tq, V_DIM), BF16)

    def update(q, keys, mask, m, acc, hh):
        k_cat = jnp.concatenate([kv_ref[0, keys, hh * 256:hh * 256 + NOPE_DIM], kr_ref[0, keys, :]], axis=1)
        v_ext = jnp.concatenate([kv_ref[0, keys, hh * 256 + NOPE_DIM:(hh + 1) * 256], ones], axis=1)
        sc = lax.dot_general(q, k_cat, _NT, preferred_element_type=F32)
        if mask is not None:
            sc = jnp.where(mask, sc, -jnp.inf)
        m_new = jnp.maximum(m, jnp.max(sc, axis=-1, keepdims=True))
        p = jnp.exp2(sc - m_new)
        acc = jnp.exp2(m - m_new) * acc + jnp.dot(p.astype(BF16), v_ext, preferred_element_type=F32)
        return m_new, acc

    for hh in range(heads):
        for qi in range(s_len // tq):
            rows = slice(qi * tq, (qi + 1) * tq)
            q = q_ref[0, rows, hh * 256:(hh + 1) * 256]
            m = jnp.full((tq, 1), -jnp.inf, F32)
            acc = jnp.zeros((tq, 2 * V_DIM), F32)
            for kj in range(qi + 1):
                m, acc = update(q, slice(kj * tq, (kj + 1) * tq), causal if kj == qi else None, m, acc, hh)
            out = acc[:, :V_DIM] / acc[:, V_DIM:] * g_ref[0, rows, hh * V_DIM:(hh + 1) * V_DIM].astype(F32)
            o_ref[0, rows, hh * V_DIM:(hh + 1) * V_DIM] = out.astype(BF16)


def _attn_call(q, kv, kr, gate, *, tq, heads):
    b, s, _ = q.shape
    kernel = functools.partial(_attn_kernel, tq=tq, heads=heads)
    return pl.pallas_call(
        kernel,
        grid=(b, MLA_HEADS // heads),
        in_specs=[pl.BlockSpec((1, s, heads * 256), lambda bi, hp: (bi, 0, hp)),
                  pl.BlockSpec((1, s, heads * 256), lambda bi, hp: (bi, 0, hp)),
                  pl.BlockSpec((1, s, LANES), lambda bi, hp: (bi, 0, 0)),
                  pl.BlockSpec((1, s, heads * V_DIM), lambda bi, hp: (bi, 0, hp))],
        out_specs=pl.BlockSpec((1, s, heads * V_DIM), lambda bi, hp: (bi, 0, hp)),
        out_shape=jax.ShapeDtypeStruct((b, s, D_INNER), BF16),
        compiler_params=_params(2),
        name="mla_flash_attention",
    )(q, kv, kr, gate)


def _rope_tables(seq):
    pos = np.arange(seq, dtype=np.float64)
    inv_freq = ROPE_THETA ** (-np.arange(0, ROPE_DIM, 2, dtype=np.float64) / ROPE_DIM)
    ang = pos[:, None] * inv_freq[None, :]
    cos, sin = np.cos(ang), np.sin(ang)
    return jnp.asarray(np.concatenate([cos, cos, -sin, sin], axis=-1), dtype=F32)


def _q_pair_selection():
    d_in, d_out = NOPE_DIM + ROPE_DIM, 2 * LANES
    sel = np.zeros((2 * d_in, 2 * d_out), np.float32)
    half = ROPE_DIM // 2
    for h in range(2):
        for c in range(d_in):
            sel[h * d_in + c, h * d_out + c] = 1.0
        for c in range(ROPE_DIM):
            sel[h * d_in + NOPE_DIM + (c + half) % ROPE_DIM, h * d_out + d_in + c] = 1.0
    return sel


def _swap_halves(w):
    half = w.shape[-1] // 2
    return jnp.concatenate([w[..., half:], w[..., :half]], axis=-1)


def kernel(x, norm_g, hg_w_in, hg_g_norm, hg_w_out, hg_lb, kv_in_norm_g, w_kv_down, kv_norm_g,
           w_kv_up, mla_w_in, mla_q_norm_g, mla_w_q_up, mla_w_out, final_norm_g):
    b, s, d = x.shape
    m = b * s
    x2 = x.reshape(m, d)
    tm, tn, sub_n = MM_TM, MM_TN, MM_SUB_N
    rope_table = _rope_tables(s)

    def rope_spec(rows):
        return pl.BlockSpec((rows, LANES), lambda j, i: (i % (s // rows), 0))

    def project(xb, w, n, epilogue, out_dtypes, name, *, col0=0, rows=tm, n_tile=tn, group=sub_n,
                row_group=MM_SUB_M, extras=(), extra_specs=()):
        outs = _mm_call(xb, w, n=n, col0=col0, tm=rows, tn=n_tile, sub_m=row_group, sub_n=group,
                        epilogue=epilogue,
                        extras=extras, extra_specs=extra_specs,
                        out_shapes=[jax.ShapeDtypeStruct((m, n), dt) for dt in out_dtypes],
                        out_specs=[pl.BlockSpec((rows, n_tile), lambda j, i: (i, j)) for _ in out_dtypes],
                        name=name)
        return outs if len(outs) > 1 else outs[0]

    xn = _norm_cast_call(x2, norm_g[0].reshape(1, d), tm=1024, name="hg_norm")
    w_in = hg_w_in[0]
    c1, c2, c3 = HG_KEY_TOTAL, 2 * HG_KEY_TOTAL, 2 * HG_KEY_TOTAL + D_INNER
    lf, kg = project(xn, w_in, c2 - c1, _ep_forget_gate, [F32, BF16], "hg_in_f", col0=c1, n_tile=tn // 2,
                     row_group=tm // 2, extras=(hg_lb,),
                     extra_specs=(pl.BlockSpec((hg_lb.shape[0], tn // 2), lambda j, i: (0, j)),))
    f_tiles = (c2 - c1) // tn
    n_qig = w_in.shape[1] - (c2 - c1)
    qig = _mm_call(xn, w_in, n=n_qig, tm=tm, tn=tn, sub_m=MM_SUB_M, sub_n=sub_n,
                   epilogue=((0, _ep_cast), ((c3 - (c2 - c1)) // tn, _ep_silu)),
                   w_tile_of=lambda j: jnp.where(j < c1 // tn, j, j + f_tiles),
                   out_shapes=[jax.ShapeDtypeStruct((m, n_qig), BF16)],
                   out_specs=[pl.BlockSpec((tm, tn), lambda j, i: (i, j))], name="hg_in_qig")[0]
    qig = qig.reshape(b, s, -1)

    y = _hgrn_call(qig, lf.reshape(b, s, -1), kg.reshape(b, s, -1), qig, qig,
                   hg_g_norm[0].reshape(1, HG_VAL_DIM), hb=4, ts=512,
                   col_offsets=(0, c1, c1 + (c3 - c2)))
    h1, h1n_kv, h1n_q = _mm_res_call(y.reshape(m, D_INNER), hg_w_out[0].astype(BF16), x2,
                                     (kv_in_norm_g.reshape(1, d), norm_g[1].reshape(1, d)),
                                     final_norm=False, tm=RES_TM, sub_n=RES_SUB_N, name="hg_out")

    w_kvd = jnp.concatenate([w_kv_down, _swap_halves(w_kv_down[:, KV_LORA:])], axis=-1)
    n_kvd = w_kvd.shape[1]
    cn, kr = _mm_call(h1n_kv, w_kvd, n=n_kvd, tm=1024, tn=n_kvd, sub_m=MM_SUB_M, sub_n=n_kvd,
                      epilogue=_ep_kv_down,
                      extras=(kv_norm_g.reshape(1, KV_LORA), rope_table),
                      extra_specs=(pl.BlockSpec((1, KV_LORA), lambda j, i: (0, 0)), rope_spec(1024)),
                      out_shapes=[jax.ShapeDtypeStruct((m, KV_LORA), BF16),
                                  jax.ShapeDtypeStruct((m, LANES), BF16)],
                      out_specs=[pl.BlockSpec((1024, KV_LORA), lambda j, i: (i, 0)),
                                 pl.BlockSpec((1024, LANES), lambda j, i: (i, 0))], name="kv_down")
    kv = project(cn, w_kv_up, w_kv_up.shape[1], _ep_cast, [BF16], "kv_up", n_tile=2 * tn)

    w_in1 = mla_w_in[0]
    cqn = _mm_call(h1n_q, w_in1, n=Q_LORA, tm=1024, tn=Q_LORA, sub_m=MM_SUB_M, sub_n=Q_LORA,
                   epilogue=_ep_rms,
                   extras=(mla_q_norm_g[0].reshape(1, Q_LORA),),
                   extra_specs=(pl.BlockSpec((1, Q_LORA), lambda j, i: (0, 0)),),
                   out_shapes=[jax.ShapeDtypeStruct((m, Q_LORA), BF16)],
                   out_specs=[pl.BlockSpec((1024, Q_LORA), lambda j, i: (i, 0))], name="mla_in_q")[0]
    gate = project(h1n_q, w_in1, w_in1.shape[1] - Q_LORA, _ep_silu, [BF16], "mla_in_gate", col0=Q_LORA)

    sel = jnp.asarray(_q_pair_selection(), dtype=BF16)
    heads_per_tile = 2 * tn // (2 * LANES)
    n_q = MLA_HEADS * 2 * LANES
    qf = _mm_call(cqn, mla_w_q_up[0], n=n_q, tm=tm, tn=2 * tn, w_tn=heads_per_tile * (NOPE_DIM + ROPE_DIM),
                  sub_m=MM_SUB_M, sub_n=512, epilogue=_ep_q_up, prep=_prep_q_up,
                  extras=(rope_table, sel),
                  extra_specs=(rope_spec(tm), pl.BlockSpec(sel.shape, lambda j, i: (0, 0))),
                  out_shapes=[jax.ShapeDtypeStruct((m, n_q), BF16)],
                  out_specs=[pl.BlockSpec((tm, 2 * tn), lambda j, i: (i, j))], name="mla_q_up")[0]

    attn = _attn_call(qf.reshape(b, s, -1), kv.reshape(b, s, -1), kr.reshape(b, s, -1),
                      gate.reshape(b, s, -1), tq=512, heads=4)
    out = _mm_res_call(attn.reshape(m, D_INNER), mla_w_out[0].astype(BF16), h1,
                       (final_norm_g.reshape(1, d),), final_norm=True, tm=RES_TM, sub_n=RES_SUB_N,
                       name="mla_out")[0]
    return out.reshape(b, s, d)
```

```python
import functools

import numpy as np
import jax
import jax.numpy as jnp
from jax import lax
from jax.experimental import pallas as pl
from jax.experimental.pallas import tpu as pltpu

F32 = jnp.float32
BF16 = jnp.bfloat16

D_MODEL = 2048
D_INNER = 2 * D_MODEL
HG_KEY_DIM = 128
HG_HEADS = D_MODEL // HG_KEY_DIM
HG_KEY_TOTAL = HG_HEADS * HG_KEY_DIM
HG_VAL_DIM = D_INNER // HG_HEADS
HG_CHUNK = 64
HG_LEVELS = 6
MLA_HEADS = 32
Q_LORA = 768
KV_LORA = 512
NOPE_DIM = 128
ROPE_DIM = 64
V_DIM = 128
ROPE_THETA = 10000.0
EPS = 1e-6
LOG2_E = 1.4426950408889634

LANES = 128
VMEM_LIMIT = 56 * 1024 * 1024

MM_TM, MM_TN, MM_SUB_M, MM_SUB_N = 2048, 1024, 128, 512
CAST_ROWS = 256
RES_TM, RES_SUB_N = 512, 512

_NT = (((1,), (1,)), ((), ()))
_TN = (((0,), (0,)), ((), ()))


def _params(n_axes):
    return pltpu.CompilerParams(dimension_semantics=("arbitrary",) * n_axes,
                                vmem_limit_bytes=VMEM_LIMIT)


def _rms(x):
    return x * lax.rsqrt(jnp.mean(x * x, axis=-1, keepdims=True) + EPS)


def _silu(x):
    return x / (1.0 + jnp.exp(-x))


def _norm_cast_kernel(x_ref, g_ref, o_ref):
    o_ref[...] = (_rms(x_ref[...]) * g_ref[...]).astype(BF16)


def _norm_cast_call(x, g, *, tm, name):
    m, k = x.shape
    return pl.pallas_call(
        _norm_cast_kernel,
        grid=(m // tm,),
        in_specs=[pl.BlockSpec((tm, k), lambda i: (i, 0)), pl.BlockSpec((1, k), lambda i: (0, 0))],
        out_specs=pl.BlockSpec((tm, k), lambda i: (i, 0)),
        out_shape=jax.ShapeDtypeStruct((m, k), BF16),
        compiler_params=_params(1),
        name=name,
    )(x, g)


def _prep_cast(w_ref, wb_ref, extras):
    def cast_rows(kk, carry):
        rows = pl.ds(pl.multiple_of(kk * CAST_ROWS, CAST_ROWS), CAST_ROWS)
        wb_ref[rows, :] = w_ref[rows, :].astype(BF16)
        return carry
    lax.fori_loop(0, w_ref.shape[0] // CAST_ROWS, cast_rows, 0)


def _prep_q_up(w_ref, wb_ref, extras):
    sel = extras[-1][...]
    n_in, n_out = sel.shape
    for p in range(w_ref.shape[1] // n_in):
        pair = w_ref[:, p * n_in:(p + 1) * n_in].astype(BF16)
        wb_ref[:, p * n_out:(p + 1) * n_out] = jnp.dot(pair, sel, preferred_element_type=F32).astype(BF16)


def _mm_kernel(epilogue, n_extra, sub_m, sub_n, prep, x_ref, w_ref, *refs):
    extras = refs[:n_extra]
    if prep is not None:
        outs, wb_ref = refs[n_extra:-1], refs[-1]

        @pl.when(pl.program_id(1) == 0)
        def _():
            prep(w_ref, wb_ref, extras)
    else:
        outs, wb_ref = refs[n_extra:], w_ref

    def tile(ep):
        for c0 in range(0, wb_ref.shape[1], sub_n):
            cols = slice(c0, c0 + sub_n)
            for r0 in range(0, x_ref.shape[0], sub_m):
                rows = slice(r0, r0 + sub_m)
                acc = jnp.dot(x_ref[rows, :], wb_ref[:, cols], preferred_element_type=F32)
                ep(acc, rows, cols, extras, outs)

    if callable(epilogue):
        tile(epilogue)
    else:
        j = pl.program_id(0)
        firsts = [first for first, _ in epilogue] + [pl.num_programs(0)]
        for v, (first, ep) in enumerate(epilogue):
            pl.when((j >= first) & (j < firsts[v + 1]))(functools.partial(tile, ep))


def _mm_call(x, w, *, n, col0=0, tm, tn, sub_m, sub_n, epilogue, extras=(), extra_specs=(), out_shapes,
             out_specs, name, prep=None, w_tn=None, w_tile_of=None):
    m, k = x.shape
    if prep is None and w.dtype != BF16:
        prep = _prep_cast
    w_tn = tn if w_tn is None else w_tn
    if w_tile_of is not None:
        w_spec = pl.BlockSpec((k, w_tn), lambda j, i: (0, w_tile_of(j)))
    elif col0 % w_tn == 0:
        w_spec = pl.BlockSpec((k, w_tn), lambda j, i: (0, j + col0 // w_tn))
    else:
        assert col0 % LANES == 0
        w_spec = pl.BlockSpec((pl.Element(k), pl.Element(w_tn)),
                              lambda j, i: (0, pl.multiple_of(col0 + j * w_tn, LANES)))
    kernel = functools.partial(_mm_kernel, epilogue, len(extras), sub_m, sub_n, prep)
    return pl.pallas_call(
        kernel,
        grid=(n // tn, m // tm),
        in_specs=[pl.BlockSpec((tm, k), lambda j, i: (i, 0)), w_spec] + list(extra_specs),
        out_specs=out_specs,
        out_shape=out_shapes,
        scratch_shapes=[pltpu.VMEM((k, tn), BF16)] if prep is not None else [],
        compiler_params=_params(2),
        name=name,
    )(x, w, *extras)


def _ep_cast(acc, rows, cols, extras, outs):
    outs[0][rows, cols] = acc.astype(outs[0].dtype)


def _ep_silu(acc, rows, cols, extras, outs):
    outs[0][rows, cols] = _silu(acc).astype(outs[0].dtype)


def _ep_forget_gate(acc, rows, cols, extras, outs):
    lb_logits = extras[0][:, cols]
    mx = jnp.max(lb_logits, axis=0, keepdims=True)
    e = jnp.exp(lb_logits - mx)
    lb = e[0:1, :] / jnp.sum(e, axis=0, keepdims=True)
    t = jnp.exp(-jnp.abs(acc))
    r = 1.0 / (1.0 + t)
    pos = acc >= 0
    sig = jnp.where(pos, r, t * r)
    sig_neg = jnp.where(pos, t * r, r)
    outs[0][rows, cols] = jnp.log(lb + (1.0 - lb) * sig)
    outs[1][rows, cols] = ((1.0 - lb) * sig_neg).astype(BF16)


def _ep_rms(acc, rows, cols, extras, outs):
    outs[0][rows, :] = (_rms(acc) * extras[0][...]).astype(BF16)


def _rotary(x, table):
    y = x * table
    return y + pltpu.roll(y, ROPE_DIM, axis=1)


def _ep_kv_down(acc, rows, cols, extras, outs):
    gain, table = extras
    outs[0][rows, :] = (_rms(acc[:, :KV_LORA]) * gain[...]).astype(BF16)
    outs[1][rows, :] = _rotary(acc[:, KV_LORA:], table[rows, :]).astype(BF16)


def _ep_q_up(acc, rows, cols, extras, outs):
    table = extras[0]
    scale = (NOPE_DIM + ROPE_DIM) ** -0.5 * LOG2_E
    lane = lax.broadcasted_iota(jnp.int32, (acc.shape[0], LANES), 1)
    base = cols.start
    for h in range(2):
        keep = (lane < ROPE_DIM) if h == 0 else (lane >= ROPE_DIM)
        c0 = 256 * h
        rope = _rotary(acc[:, c0 + 128:c0 + 256], table[rows, :]) * scale
        outs[0][rows, base + c0:base + c0 + 128] = (acc[:, c0:c0 + 128] * scale).astype(BF16)
        outs[0][rows, base + c0 + 128:base + c0 + 256] = jnp.where(keep, rope, 0.0).astype(BF16)


def _mm_res_kernel(final_norm, sub_n, y_ref, w_ref, r_ref, *refs):
    if final_norm:
        g_ref, o_ref = refs
    else:
        ga_ref, gb_ref, o_ref, na_ref, nb_ref = refs
    n = w_ref.shape[1]
    ssq = jnp.zeros((y_ref.shape[0], 1), F32)
    for c0 in range(0, n, sub_n):
        cols = slice(c0, c0 + sub_n)
        h = r_ref[:, cols] + jnp.dot(y_ref[...], w_ref[:, cols], preferred_element_type=F32)
        o_ref[:, cols] = h
        ssq = ssq + jnp.sum(h * h, axis=-1, keepdims=True)
    inv = lax.rsqrt(ssq * (1.0 / n) + EPS)
    for c0 in range(0, n, sub_n):
        cols = slice(c0, c0 + sub_n)
        hn = o_ref[:, cols] * inv
        if final_norm:
            o_ref[:, cols] = hn * g_ref[:, cols]
        else:
            na_ref[:, cols] = (hn * ga_ref[:, cols]).astype(BF16)
            nb_ref[:, cols] = (hn * gb_ref[:, cols]).astype(BF16)


def _mm_res_call(y, w, res, gains, *, final_norm, tm, sub_n, name):
    m, kdim = y.shape
    n = w.shape[1]
    row_spec = pl.BlockSpec((tm, n), lambda i: (i, 0))
    gain_spec = pl.BlockSpec((1, n), lambda i: (0, 0))
    out_shape = [jax.ShapeDtypeStruct((m, n), F32)]
    out_specs = [row_spec]
    if not final_norm:
        out_shape += [jax.ShapeDtypeStruct((m, n), BF16)] * 2
        out_specs += [row_spec, row_spec]
    return pl.pallas_call(
        functools.partial(_mm_res_kernel, final_norm, sub_n),
        grid=(m // tm,),
        in_specs=[pl.BlockSpec((tm, kdim), lambda i: (i, 0)),
                  pl.BlockSpec((kdim, n), lambda i: (0, 0), pipeline_mode=pl.Buffered(1)),
                  row_spec] + [gain_spec] * len(gains),
        out_specs=out_specs,
        out_shape=out_shape,
        compiler_params=_params(1),
        name=name,
    )(y, w, res, *gains)


def _hgrn_exponent_matrix():
    c = HG_CHUNK
    t = np.arange(c)[:, None]
    u = np.arange(c)[None, :]
    mats = [(u <= t).astype(np.float32), (u > t).astype(np.float32)]
    for level in range(HG_LEVELS):
        m = 1 << level
        r = ((t >> (level + 1)) << (level + 1)) + m - 1
        upper = ((t >> level) & 1) == 1
        up = ((u > r) & (u <= t)).astype(np.float32)
        lo = ((u > t) & (u <= r)).astype(np.float32)
        mats.append(np.where(upper, up, lo))
    return np.concatenate(mats, axis=0)


def _hgrn_kernel(q_ref, lf_ref, k_ref, v_ref, sg_ref, gn_ref, e_ref, w_ref, y_ref, wb_ref,
                 st_ref, ex_ref, qe_ref, a_ref, inc_ref, sb_ref, *, hb, ts):
    wb_ref[...] = w_ref[...].astype(BF16)
    c = HG_CHUNK
    dk = HG_KEY_DIM
    dv = HG_VAL_DIM

    @pl.when(pl.program_id(2) == 0)
    def _():
        st_ref[...] = jnp.zeros_like(st_ref)

    nc = ts // c
    sub = 8
    e_mat = e_ref[...]
    t_i = lax.broadcasted_iota(jnp.int32, (c, c), 0)
    s_i = lax.broadcasted_iota(jnp.int32, (c, c), 1)
    diff = t_i ^ s_i
    causal = s_i < t_i
    pair_masks = [jnp.where(causal & ((diff >> level) == 1), 1.0, 0.0) for level in range(HG_LEVELS)]
    diag = jnp.where(t_i == s_i, 1.0, 0.0)
    row = lax.broadcasted_iota(jnp.int32, (c, dk), 0)
    uppers = [((row >> level) & 1) == 1 for level in range(HG_LEVELS)]
    gn = gn_ref[...]

    def decays(ci):
        lf = lf_ref[0, ci * c:(ci + 1) * c, :]
        hi = lf.astype(BF16)
        lo = (lf - hi.astype(F32)).astype(BF16)
        ex_ref[ci] = jnp.exp(jnp.dot(e_mat, jnp.concatenate([hi, lo], axis=0), preferred_element_type=F32))

    def chunk_local(ci):
        rows = slice(ci * c, (ci + 1) * c)
        for h in range(hb):
            idx = ci * hb + h
            cs = slice(h * dk, (h + 1) * dk)
            q = q_ref[0, rows, cs].astype(F32)
            k = k_ref[0, rows, cs].astype(F32)
            qe_ref[idx] = (q * ex_ref[ci, 0:c, cs]).astype(BF16)
            ks = (k * ex_ref[ci, c:2 * c, cs]).astype(BF16)
            inc_ref[idx] = lax.dot_general(ks, v_ref[0, rows, h * dv:(h + 1) * dv], _TN,
                                           preferred_element_type=F32)

            scores = diag * jnp.sum(q * k, axis=-1, keepdims=True)
            for level in range(HG_LEVELS):
                x = ex_ref[ci, (2 + level) * c:(3 + level) * c, cs]
                if (1 << level) >= sub:
                    w = jnp.concatenate(
                        [x[r0:r0 + sub] * (q if (r0 >> level) & 1 else k)[r0:r0 + sub] for r0 in range(0, c, sub)],
                        axis=0)
                else:
                    w = x * jnp.where(uppers[level], q, k)
                w = w.astype(BF16)
                scores = scores + lax.dot_general(w, w, _NT, preferred_element_type=F32) * pair_masks[level]
            a_ref[idx] = scores.astype(BF16)

    def recur(ci, states):
        new = []
        for h in range(hb):
            idx = ci * hb + h
            sb_ref[idx] = states[h].astype(BF16)
            decay_row = ex_ref[ci, c - 1:c, h * dk:(h + 1) * dk]
            decay_col = jnp.transpose(jnp.broadcast_to(decay_row, (dk, dk)))
            new.append(states[h] * jnp.concatenate([decay_col] * (dv // dk), axis=1) + inc_ref[idx])
        return new

    def outputs(ci):
        rows = slice(ci * c, (ci + 1) * c)
        for h in range(hb):
            idx = ci * hb + h
            vs = slice(h * dv, (h + 1) * dv)
            o = (jnp.dot(qe_ref[idx], sb_ref[idx], preferred_element_type=F32)
                 + jnp.dot(a_ref[idx], v_ref[0, rows, vs], preferred_element_type=F32))
            y = _rms(o) * gn * sg_ref[0, rows, vs].astype(F32)
            y_ref[0, rows, vs] = y.astype(BF16)

    states = [st_ref[h] for h in range(hb)]
    for it in range(nc + 2):
        if it < nc:
            decays(it)
        if 0 <= it - 1 < nc:
            chunk_local(it - 1)
            states = recur(it - 1, states)
        if 0 <= it - 2 < nc:
            outputs(it - 2)
    for h in range(hb):
        st_ref[h] = states[h]


def _cast_rider_specs(w, grid):
    steps = int(np.prod(grid))
    rows = w.shape[0] // steps
    strides = [int(np.prod(grid[a + 1:])) for a in range(len(grid))]
    spec = pl.BlockSpec((rows, w.shape[1]), lambda *ids: (sum(i * st for i, st in zip(ids, strides)), 0))
    return spec, jax.ShapeDtypeStruct(w.shape, BF16)


def _hgrn_call(q, lf, k, v, sg, gn, w_next, *, hb, ts, col_offsets=(0, 0, 0)):
    b, s, _ = q.shape
    e_one = _hgrn_exponent_matrix()
    e_mat = jnp.asarray(np.concatenate([e_one, e_one], axis=1), dtype=BF16)
    kernel = functools.partial(_hgrn_kernel, hb=hb, ts=ts)
    nhc = (ts // HG_CHUNK) * hb

    def spec(width, col0=0):
        return pl.BlockSpec((1, ts, width), lambda bi, hi, si: (bi, si, hi + col0 // width))

    key_w, val_w = hb * HG_KEY_DIM, hb * HG_VAL_DIM
    key_spec, val_spec = spec(key_w), spec(val_w)
    grid = (b, HG_HEADS // hb, s // ts)
    w_spec, wb_shape = _cast_rider_specs(w_next, grid)
    return pl.pallas_call(
        kernel,
        grid=grid,
        in_specs=[spec(key_w, col_offsets[0]), key_spec, key_spec, spec(val_w, col_offsets[1]),
                  spec(val_w, col_offsets[2]),
                  pl.BlockSpec((1, HG_VAL_DIM), lambda bi, hi, si: (0, 0)),
                  pl.BlockSpec(e_mat.shape, lambda bi, hi, si: (0, 0)),
                  w_spec],
        out_specs=[val_spec, w_spec],
        out_shape=[jax.ShapeDtypeStruct((b, s, D_INNER), BF16), wb_shape],
        scratch_shapes=[pltpu.VMEM((hb, HG_KEY_DIM, HG_VAL_DIM), F32),
                        pltpu.VMEM((ts // HG_CHUNK,) + (e_mat.shape[0], hb * HG_KEY_DIM), F32),
                        pltpu.VMEM((nhc, HG_CHUNK, HG_KEY_DIM), BF16),
                        pltpu.VMEM((nhc, HG_CHUNK, HG_CHUNK), BF16),
                        pltpu.VMEM((nhc, HG_KEY_DIM, HG_VAL_DIM), F32),
                        pltpu.VMEM((nhc, HG_KEY_DIM, HG_VAL_DIM), BF16)],
        compiler_params=_params(3),
        name="hgrn2_recurrence",
    )(q, lf, k, v, sg, gn, e_mat, w_next)


def _attn_kernel(q_ref, kv_ref, kr_ref, g_ref, w_ref, o_ref, wb_ref, *, tq, heads):
    wb_ref[...] = w_ref[...].astype(BF16)
    s_len = q_ref.shape[1]
    t_i = lax.broadcasted_iota(jnp.int32, (tq, tq), 0)
    s_i = lax.broadcasted_iota(jnp.int32, (tq, tq), 1)
    causal = s_i <= t_i
    ones = jnp.ones((tq, V_DIM), BF16)

    def update(q, keys, mask, m, acc, hh):
        k_cat = jnp.concatenate([kv_ref[0, keys, hh * 256:hh * 256 + NOPE_DIM], kr_ref[0, keys, :]], axis=1)
        v_ext = jnp.concatenate([kv_ref[0, keys, hh * 256 + NOPE_DIM:(hh + 1) * 256], ones], axis=1)
        sc = lax.dot_general(q, k_cat, _NT, preferred_element_type=F32)
        if mask is not None:
            sc = jnp.where(mask, sc, -jnp.inf)
        m_new = jnp.maximum(m, jnp.max(sc, axis=-1, keepdims=True))
        p = jnp.exp2(sc - m_new)
        acc = jnp.exp2(m - m_new) * acc + jnp.dot(p.astype(BF16), v_ext, preferred_element_type=F32)
        return m_new, acc

    for hh in range(heads):
        for qi in range(s_len // tq):
            rows = slice(qi * tq, (qi + 1) * tq)
            q = q_ref[0, rows, hh * 256:(hh + 1) * 256]
            m = jnp.full((tq, 1), -jnp.inf, F32)
            acc = jnp.zeros((tq, 2 * V_DIM), F32)
            for kj in range(qi + 1):
                m, acc = update(q, slice(kj * tq, (kj + 1) * tq), causal if kj == qi else None, m, acc, hh)
            out = acc[:, :V_DIM] / acc[:, V_DIM:] * g_ref[0, rows, hh * V_DIM:(hh + 1) * V_DIM].astype(F32)
            o_ref[0, rows, hh * V_DIM:(hh + 1) * V_DIM] = out.astype(BF16)


def _attn_call(q, kv, kr, gate, w_next, *, tq, heads):
    b, s, _ = q.shape
    kernel = functools.partial(_attn_kernel, tq=tq, heads=heads)
    grid = (b, MLA_HEADS // heads)
    w_spec, wb_shape = _cast_rider_specs(w_next, grid)
    out_spec = pl.BlockSpec((1, s, heads * V_DIM), lambda bi, hp: (bi, 0, hp))
    return pl.pallas_call(
        kernel,
        grid=grid,
        in_specs=[pl.BlockSpec((1, s, heads * 256), lambda bi, hp: (bi, 0, hp)),
                  pl.BlockSpec((1, s, heads * 256), lambda bi, hp: (bi, 0, hp)),
                  pl.BlockSpec((1, s, LANES), lambda bi, hp: (bi, 0, 0)),
                  out_spec, w_spec],
        out_specs=[out_spec, w_spec],
        out_shape=[jax.ShapeDtypeStruct((b, s, D_INNER), BF16), wb_shape],
        compiler_params=_params(2),
        name="mla_flash_attention",
    )(q, kv, kr, gate, w_next)


def _rope_tables(seq):
    pos = np.arange(seq, dtype=np.float64)
    inv_freq = ROPE_THETA ** (-np.arange(0, ROPE_DIM, 2, dtype=np.float64) / ROPE_DIM)
    ang = pos[:, None] * inv_freq[None, :]
    cos, sin = np.cos(ang), np.sin(ang)
    return jnp.asarray(np.concatenate([cos, cos, -sin, sin], axis=-1), dtype=F32)


def _q_pair_selection():
    d_in, d_out = NOPE_DIM + ROPE_DIM, 2 * LANES
    sel = np.zeros((2 * d_in, 2 * d_out), np.float32)
    half = ROPE_DIM // 2
    for h in range(2):
        for c in range(d_in):
            sel[h * d_in + c, h * d_out + c] = 1.0
        for c in range(ROPE_DIM):
            sel[h * d_in + NOPE_DIM + (c + half) % ROPE_DIM, h * d_out + d_in + c] = 1.0
    return sel


def _swap_halves(w):
    half = w.shape[-1] // 2
    return jnp.concatenate([w[..., half:], w[..., :half]], axis=-1)


def kernel(x, norm_g, hg_w_in, hg_g_norm, hg_w_out, hg_lb, kv_in_norm_g, w_kv_down, kv_norm_g,
           w_kv_up, mla_w_in, mla_q_norm_g, mla_w_q_up, mla_w_out, final_norm_g):
    b, s, d = x.shape
    m = b * s
    x2 = x.reshape(m, d)
    tm, tn, sub_n = MM_TM, MM_TN, MM_SUB_N
    rope_table = _rope_tables(s)

    def rope_spec(rows):
        return pl.BlockSpec((rows, LANES), lambda j, i: (i % (s // rows), 0))

    def project(xb, w, n, epilogue, out_dtypes, name, *, col0=0, rows=tm, n_tile=tn, group=sub_n,
                row_group=MM_SUB_M, extras=(), extra_specs=()):
        outs = _mm_call(xb, w, n=n, col0=col0, tm=rows, tn=n_tile, sub_m=row_group, sub_n=group,
                        epilogue=epilogue,
                        extras=extras, extra_specs=extra_specs,
                        out_shapes=[jax.ShapeDtypeStruct((m, n), dt) for dt in out_dtypes],
                        out_specs=[pl.BlockSpec((rows, n_tile), lambda j, i: (i, j)) for _ in out_dtypes],
                        name=name)
        return outs if len(outs) > 1 else outs[0]

    xn = _norm_cast_call(x2, norm_g[0].reshape(1, d), tm=1024, name="hg_norm")
    w_in = hg_w_in[0]
    c1, c2, c3 = HG_KEY_TOTAL, 2 * HG_KEY_TOTAL, 2 * HG_KEY_TOTAL + D_INNER
    lf, kg = project(xn, w_in, c2 - c1, _ep_forget_gate, [F32, BF16], "hg_in_f", col0=c1, n_tile=tn // 2,
                     row_group=tm // 2, extras=(hg_lb,),
                     extra_specs=(pl.BlockSpec((hg_lb.shape[0], tn // 2), lambda j, i: (0, j)),))
    f_tiles = (c2 - c1) // tn
    n_qig = w_in.shape[1] - (c2 - c1)
    qig = _mm_call(xn, w_in, n=n_qig, tm=tm, tn=tn, sub_m=MM_SUB_M, sub_n=sub_n,
                   epilogue=((0, _ep_cast), ((c3 - (c2 - c1)) // tn, _ep_silu)),
                   w_tile_of=lambda j: jnp.where(j < c1 // tn, j, j + f_tiles),
                   out_shapes=[jax.ShapeDtypeStruct((m, n_qig), BF16)],
                   out_specs=[pl.BlockSpec((tm, tn), lambda j, i: (i, j))], name="hg_in_qig")[0]
    qig = qig.reshape(b, s, -1)

    y, w_out0 = _hgrn_call(qig, lf.reshape(b, s, -1), kg.reshape(b, s, -1), qig, qig,
                           hg_g_norm[0].reshape(1, HG_VAL_DIM), hg_w_out[0], hb=2, ts=1024,
                           col_offsets=(0, c1, c1 + (c3 - c2)))
    h1, h1n_kv, h1n_q = _mm_res_call(y.reshape(m, D_INNER), w_out0, x2,
                                     (kv_in_norm_g.reshape(1, d), norm_g[1].reshape(1, d)),
                                     final_norm=False, tm=RES_TM, sub_n=RES_SUB_N, name="hg_out")

    w_kvd = jnp.concatenate([w_kv_down, _swap_halves(w_kv_down[:, KV_LORA:])], axis=-1)
    n_kvd = w_kvd.shape[1]
    cn, kr = _mm_call(h1n_kv, w_kvd, n=n_kvd, tm=1024, tn=n_kvd, sub_m=MM_SUB_M, sub_n=n_kvd,
                      epilogue=_ep_kv_down,
                      extras=(kv_norm_g.reshape(1, KV_LORA), rope_table),
                      extra_specs=(pl.BlockSpec((1, KV_LORA), lambda j, i: (0, 0)), rope_spec(1024)),
                      out_shapes=[jax.ShapeDtypeStruct((m, KV_LORA), BF16),
                                  jax.ShapeDtypeStruct((m, LANES), BF16)],
                      out_specs=[pl.BlockSpec((1024, KV_LORA), lambda j, i: (i, 0)),
                                 pl.BlockSpec((1024, LANES), lambda j, i: (i, 0))], name="kv_down")
    kv = project(cn, w_kv_up, w_kv_up.shape[1], _ep_cast, [BF16], "kv_up", n_tile=2 * tn)

    w_in1 = mla_w_in[0]
    cqn = _mm_call(h1n_q, w_in1, n=Q_LORA, tm=1024, tn=Q_LORA, sub_m=MM_SUB_M, sub_n=Q_LORA,
                   epilogue=_ep_rms,
                   extras=(mla_q_norm_g[0].reshape(1, Q_LORA),),
                   extra_specs=(pl.BlockSpec((1, Q_LORA), lambda j, i: (0, 0)),),
                   out_shapes=[jax.ShapeDtypeStruct((m, Q_LORA), BF16)],
                   out_specs=[pl.BlockSpec((1024, Q_LORA), lambda j, i: (i, 0))], name="mla_in_q")[0]
    gate = project(h1n_q, w_in1, w_in1.shape[1] - Q_LORA, _ep_silu, [BF16], "mla_in_gate", col0=Q_LORA)

    sel = jnp.asarray(_q_pair_selection(), dtype=BF16)
    heads_per_tile = 2 * tn // (2 * LANES)
    n_q = MLA_HEADS * 2 * LANES
    qf = _mm_call(cqn, mla_w_q_up[0], n=n_q, tm=tm, tn=2 * tn, w_tn=heads_per_tile * (NOPE_DIM + ROPE_DIM),
                  sub_m=MM_SUB_M, sub_n=512, epilogue=_ep_q_up, prep=_prep_q_up,
                  extras=(rope_table, sel),
                  extra_specs=(rope_spec(tm), pl.BlockSpec(sel.shape, lambda j, i: (0, 0))),
                  out_shapes=[jax.ShapeDtypeStruct((m, n_q), BF16)],
                  out_specs=[pl.BlockSpec((tm, 2 * tn), lambda j, i: (i, j))], name="mla_q_up")[0]

    attn, w_out1 = _attn_call(qf.reshape(b, s, -1), kv.reshape(b, s, -1), kr.reshape(b, s, -1),
                              gate.reshape(b, s, -1), mla_w_out[0], tq=512, heads=4)
    out = _mm_res_call(attn.reshape(m, D_INNER), w_out1, h1,
                       (final_norm_g.reshape(1, d),), final_norm=True, tm=RES_TM, sub_n=RES_SUB_N,
                       name="mla_out")[0]
    return out.reshape(b, s, d)
```

```python
import functools

import numpy as np
import jax
import jax.numpy as jnp
from jax import lax
from jax.experimental import pallas as pl
from jax.experimental.pallas import tpu as pltpu

F32 = jnp.float32
BF16 = jnp.bfloat16

D_MODEL = 2048
D_INNER = 2 * D_MODEL
HG_KEY_DIM = 128
HG_HEADS = D_MODEL // HG_KEY_DIM
HG_KEY_TOTAL = HG_HEADS * HG_KEY_DIM
HG_VAL_DIM = D_INNER // HG_HEADS
HG_CHUNK = 64
HG_LEVELS = 6
MLA_HEADS = 32
Q_LORA = 768
KV_LORA = 512
NOPE_DIM = 128
ROPE_DIM = 64
V_DIM = 128
ROPE_THETA = 10000.0
EPS = 1e-6
LOG2_E = 1.4426950408889634

LANES = 128
VMEM_LIMIT = 56 * 1024 * 1024

MM_TM, MM_TN, MM_SUB_M, MM_SUB_N = 2048, 1024, 128, 512
CAST_ROWS = 256
RES_TM, RES_SUB_M, RES_SUB_N = 512, 256, 512

_NT = (((1,), (1,)), ((), ()))
_TN = (((0,), (0,)), ((), ()))


def _params(n_axes):
    return pltpu.CompilerParams(dimension_semantics=("arbitrary",) * n_axes,
                                vmem_limit_bytes=VMEM_LIMIT)


def _rms(x):
    return x * lax.rsqrt(jnp.mean(x * x, axis=-1, keepdims=True) + EPS)


def _silu(x):
    return x / (1.0 + jnp.exp(-x))


def _norm_cast_kernel(x_ref, g_ref, o_ref):
    o_ref[...] = (_rms(x_ref[...]) * g_ref[...]).astype(BF16)


def _norm_cast_call(x, g, *, tm, name):
    m, k = x.shape
    return pl.pallas_call(
        _norm_cast_kernel,
        grid=(m // tm,),
        in_specs=[pl.BlockSpec((tm, k), lambda i: (i, 0)), pl.BlockSpec((1, k), lambda i: (0, 0))],
        out_specs=pl.BlockSpec((tm, k), lambda i: (i, 0)),
        out_shape=jax.ShapeDtypeStruct((m, k), BF16),
        compiler_params=_params(1),
        name=name,
    )(x, g)


def _prep_cast(w_ref, wb_ref, extras):
    def cast_rows(kk, carry):
        rows = pl.ds(pl.multiple_of(kk * CAST_ROWS, CAST_ROWS), CAST_ROWS)
        wb_ref[rows, :] = w_ref[rows, :].astype(BF16)
        return carry
    lax.fori_loop(0, w_ref.shape[0] // CAST_ROWS, cast_rows, 0)


def _prep_q_up(w_ref, wb_ref, extras):
    sel = extras[-1][...]
    n_in, n_out = sel.shape
    for p in range(w_ref.shape[1] // n_in):
        pair = w_ref[:, p * n_in:(p + 1) * n_in].astype(BF16)
        wb_ref[:, p * n_out:(p + 1) * n_out] = jnp.dot(pair, sel, preferred_element_type=F32).astype(BF16)


def _mm_kernel(epilogue, n_extra, sub_m, sub_n, prep, x_ref, w_ref, *refs):
    extras = refs[:n_extra]
    if prep is not None:
        outs, wb_ref = refs[n_extra:-1], refs[-1]

        @pl.when(pl.program_id(1) == 0)
        def _():
            prep(w_ref, wb_ref, extras)
    else:
        outs, wb_ref = refs[n_extra:], w_ref

    def tile(ep):
        for c0 in range(0, wb_ref.shape[1], sub_n):
            cols = slice(c0, c0 + sub_n)
            for r0 in range(0, x_ref.shape[0], sub_m):
                rows = slice(r0, r0 + sub_m)
                acc = jnp.dot(x_ref[rows, :], wb_ref[:, cols], preferred_element_type=F32)
                ep(acc, rows, cols, extras, outs)

    if callable(epilogue):
        tile(epilogue)
    else:
        j = pl.program_id(0)
        firsts = [first for first, _ in epilogue] + [pl.num_programs(0)]
        for v, (first, ep) in enumerate(epilogue):
            pl.when((j >= first) & (j < firsts[v + 1]))(functools.partial(tile, ep))


def _mm_call(x, w, *, n, col0=0, tm, tn, sub_m, sub_n, epilogue, extras=(), extra_specs=(), out_shapes,
             out_specs, name, prep=None, w_tn=None, w_tile_of=None):
    m, k = x.shape
    if prep is None and w.dtype != BF16:
        prep = _prep_cast
    w_tn = tn if w_tn is None else w_tn
    if w_tile_of is not None:
        w_spec = pl.BlockSpec((k, w_tn), lambda j, i: (0, w_tile_of(j)))
    elif col0 % w_tn == 0:
        w_spec = pl.BlockSpec((k, w_tn), lambda j, i: (0, j + col0 // w_tn))
    else:
        assert col0 % LANES == 0
        w_spec = pl.BlockSpec((pl.Element(k), pl.Element(w_tn)),
                              lambda j, i: (0, pl.multiple_of(col0 + j * w_tn, LANES)))
    kernel = functools.partial(_mm_kernel, epilogue, len(extras), sub_m, sub_n, prep)
    return pl.pallas_call(
        kernel,
        grid=(n // tn, m // tm),
        in_specs=[pl.BlockSpec((tm, k), lambda j, i: (i, 0)), w_spec] + list(extra_specs),
        out_specs=out_specs,
        out_shape=out_shapes,
        scratch_shapes=[pltpu.VMEM((k, tn), BF16)] if prep is not None else [],
        compiler_params=_params(2),
        name=name,
    )(x, w, *extras)


def _ep_cast(acc, rows, cols, extras, outs):
    outs[0][rows, cols] = acc.astype(outs[0].dtype)


def _ep_silu(acc, rows, cols, extras, outs):
    outs[0][rows, cols] = _silu(acc).astype(outs[0].dtype)


def _ep_forget_gate(acc, rows, cols, extras, outs):
    lb_logits = extras[0][:, cols]
    mx = jnp.max(lb_logits, axis=0, keepdims=True)
    e = jnp.exp(lb_logits - mx)
    lb = e[0:1, :] / jnp.sum(e, axis=0, keepdims=True)
    t = jnp.exp(-jnp.abs(acc))
    r = 1.0 / (1.0 + t)
    pos = acc >= 0
    sig = jnp.where(pos, r, t * r)
    sig_neg = jnp.where(pos, t * r, r)
    outs[0][rows, cols] = jnp.log(lb + (1.0 - lb) * sig)
    outs[1][rows, cols] = ((1.0 - lb) * sig_neg).astype(BF16)


def _ep_rms(acc, rows, cols, extras, outs):
    outs[0][rows, :] = (_rms(acc) * extras[0][...]).astype(BF16)


def _rotary(x, table):
    y = x * table
    return y + pltpu.roll(y, ROPE_DIM, axis=1)


def _ep_kv_down(acc, rows, cols, extras, outs):
    gain, table = extras
    outs[0][rows, :] = (_rms(acc[:, :KV_LORA]) * gain[...]).astype(BF16)
    outs[1][rows, :] = _rotary(acc[:, KV_LORA:], table[rows, :]).astype(BF16)


def _ep_q_up(acc, rows, cols, extras, outs):
    table = extras[0]
    scale = (NOPE_DIM + ROPE_DIM) ** -0.5 * LOG2_E
    lane = lax.broadcasted_iota(jnp.int32, (acc.shape[0], LANES), 1)
    base = cols.start
    for h in range(2):
        keep = (lane < ROPE_DIM) if h == 0 else (lane >= ROPE_DIM)
        c0 = 256 * h
        rope = _rotary(acc[:, c0 + 128:c0 + 256], table[rows, :]) * scale
        outs[0][rows, base + c0:base + c0 + 128] = (acc[:, c0:c0 + 128] * scale).astype(BF16)
        outs[0][rows, base + c0 + 128:base + c0 + 256] = jnp.where(keep, rope, 0.0).astype(BF16)


def _mm_res_kernel(final_norm, sub_n, y_ref, w_ref, r_ref, *refs):
    if final_norm:
        g_ref, o_ref = refs
    else:
        ga_ref, gb_ref, o_ref, na_ref, nb_ref = refs
    n = w_ref.shape[1]
    for r0 in range(0, y_ref.shape[0], RES_SUB_M):
        rows = slice(r0, r0 + RES_SUB_M)
        ssq = jnp.zeros((RES_SUB_M, 1), F32)
        hs = []
        for c0 in range(0, n, sub_n):
            cols = slice(c0, c0 + sub_n)
            h = r_ref[rows, cols] + jnp.dot(y_ref[rows, :], w_ref[:, cols], preferred_element_type=F32)
            hs.append(h)
            ssq = ssq + jnp.sum(h * h, axis=-1, keepdims=True)
        inv = lax.rsqrt(ssq * (1.0 / n) + EPS)
        for h, c0 in zip(hs, range(0, n, sub_n)):
            cols = slice(c0, c0 + sub_n)
            hn = h * inv
            if final_norm:
                o_ref[rows, cols] = hn * g_ref[:, cols]
            else:
                o_ref[rows, cols] = h
                na_ref[rows, cols] = (hn * ga_ref[:, cols]).astype(BF16)
                nb_ref[rows, cols] = (hn * gb_ref[:, cols]).astype(BF16)


def _mm_res_call(y, w, res, gains, *, final_norm, tm, sub_n, name):
    m, kdim = y.shape
    n = w.shape[1]
    row_spec = pl.BlockSpec((tm, n), lambda i: (i, 0))
    gain_spec = pl.BlockSpec((1, n), lambda i: (0, 0))
    out_shape = [jax.ShapeDtypeStruct((m, n), F32)]
    out_specs = [row_spec]
    if not final_norm:
        out_shape += [jax.ShapeDtypeStruct((m, n), BF16)] * 2
        out_specs += [row_spec, row_spec]
    return pl.pallas_call(
        functools.partial(_mm_res_kernel, final_norm, sub_n),
        grid=(m // tm,),
        in_specs=[pl.BlockSpec((tm, kdim), lambda i: (i, 0)),
                  pl.BlockSpec((kdim, n), lambda i: (0, 0), pipeline_mode=pl.Buffered(1)),
                  row_spec] + [gain_spec] * len(gains),
        out_specs=out_specs,
        out_shape=out_shape,
        compiler_params=_params(1),
        name=name,
    )(y, w, res, *gains)


def _hgrn_exponent_matrix():
    c = HG_CHUNK
    t = np.arange(c)[:, None]
    u = np.arange(c)[None, :]
    mats = [(u <= t).astype(np.float32), (u > t).astype(np.float32)]
    for level in range(HG_LEVELS):
        m = 1 << level
        r = ((t >> (level + 1)) << (level + 1)) + m - 1
        upper = ((t >> level) & 1) == 1
        up = ((u > r) & (u <= t)).astype(np.float32)
        lo = ((u > t) & (u <= r)).astype(np.float32)
        mats.append(np.where(upper, up, lo))
    return np.concatenate(mats, axis=0)


def _hgrn_kernel(q_ref, lf_ref, k_ref, v_ref, sg_ref, gn_ref, e_ref, w_ref, y_ref, wb_ref,
                 st_ref, ex_ref, qe_ref, a_ref, inc_ref, sb_ref, *, hb, ts):
    wb_ref[...] = w_ref[...].astype(BF16)
    c = HG_CHUNK
    dk = HG_KEY_DIM
    dv = HG_VAL_DIM

    @pl.when(pl.program_id(2) == 0)
    def _():
        st_ref[...] = jnp.zeros_like(st_ref)

    nc = ts // c
    sub = 8
    e_mat = e_ref[...]
    t_i = lax.broadcasted_iota(jnp.int32, (c, c), 0)
    s_i = lax.broadcasted_iota(jnp.int32, (c, c), 1)
    diff = t_i ^ s_i
    causal = s_i < t_i
    pair_masks = [jnp.where(causal & ((diff >> level) == 1), 1.0, 0.0) for level in range(HG_LEVELS)]
    diag = jnp.where(t_i == s_i, 1.0, 0.0)
    row = lax.broadcasted_iota(jnp.int32, (c, dk), 0)
    uppers = [((row >> level) & 1) == 1 for level in range(HG_LEVELS)]
    gn = gn_ref[...]

    def decays(ci):
        lf = lf_ref[0, ci * c:(ci + 1) * c, :]
        hi = lf.astype(BF16)
        lo = (lf - hi.astype(F32)).astype(BF16)
        ex_ref[ci] = jnp.exp(jnp.dot(e_mat, jnp.concatenate([hi, lo], axis=0), preferred_element_type=F32))

    def chunk_local(ci):
        rows = slice(ci * c, (ci + 1) * c)
        for h in range(hb):
            idx = ci * hb + h
            cs = slice(h * dk, (h + 1) * dk)
            q = q_ref[0, rows, cs].astype(F32)
            k = k_ref[0, rows, cs].astype(F32)
            qe_ref[idx] = (q * ex_ref[ci, 0:c, cs]).astype(BF16)
            ks = (k * ex_ref[ci, c:2 * c, cs]).astype(BF16)
            inc_ref[idx] = lax.dot_general(ks, v_ref[0, rows, h * dv:(h + 1) * dv], _TN,
                                           preferred_element_type=F32)

            scores = diag * jnp.sum(q * k, axis=-1, keepdims=True)
            for level in range(HG_LEVELS):
                x = ex_ref[ci, (2 + level) * c:(3 + level) * c, cs]
                if (1 << level) >= sub:
                    w = jnp.concatenate(
                        [x[r0:r0 + sub] * (q if (r0 >> level) & 1 else k)[r0:r0 + sub] for r0 in range(0, c, sub)],
                        axis=0)
                else:
                    w = x * jnp.where(uppers[level], q, k)
                w = w.astype(BF16)
                scores = scores + lax.dot_general(w, w, _NT, preferred_element_type=F32) * pair_masks[level]
            a_ref[idx] = scores.astype(BF16)

    def recur(ci, states):
        new = []
        for h in range(hb):
            idx = ci * hb + h
            sb_ref[idx] = states[h].astype(BF16)
            decay_row = ex_ref[ci, c - 1:c, h * dk:(h + 1) * dk]
            decay_col = jnp.transpose(jnp.broadcast_to(decay_row, (dk, dk)))
            new.append(states[h] * jnp.concatenate([decay_col] * (dv // dk), axis=1) + inc_ref[idx])
        return new

    def outputs(ci):
        rows = slice(ci * c, (ci + 1) * c)
        for h in range(hb):
            idx = ci * hb + h
            vs = slice(h * dv, (h + 1) * dv)
            o = (jnp.dot(qe_ref[idx], sb_ref[idx], preferred_element_type=F32)
                 + jnp.dot(a_ref[idx], v_ref[0, rows, vs], preferred_element_type=F32))
            y = _rms(o) * gn * sg_ref[0, rows, vs].astype(F32)
            y_ref[0, rows, vs] = y.astype(BF16)

    states = [st_ref[h] for h in range(hb)]
    for it in range(nc + 2):
        if it < nc:
            decays(it)
        if 0 <= it - 1 < nc:
            chunk_local(it - 1)
            states = recur(it - 1, states)
        if 0 <= it - 2 < nc:
            outputs(it - 2)
    for h in range(hb):
        st_ref[h] = states[h]


def _cast_rider_specs(w, grid):
    steps = int(np.prod(grid))
    rows = w.shape[0] // steps
    strides = [int(np.prod(grid[a + 1:])) for a in range(len(grid))]
    spec = pl.BlockSpec((rows, w.shape[1]), lambda *ids: (sum(i * st for i, st in zip(ids, strides)), 0))
    return spec, jax.ShapeDtypeStruct(w.shape, BF16)


def _hgrn_call(q, lf, k, v, sg, gn, w_next, *, hb, ts, col_offsets=(0, 0, 0)):
    b, s, _ = q.shape
    e_one = _hgrn_exponent_matrix()
    e_mat = jnp.asarray(np.concatenate([e_one, e_one], axis=1), dtype=BF16)
    kernel = functools.partial(_hgrn_kernel, hb=hb, ts=ts)
    nhc = (ts // HG_CHUNK) * hb

    def spec(width, col0=0):
        return pl.BlockSpec((1, ts, width), lambda bi, hi, si: (bi, si, hi + col0 // width))

    key_w, val_w = hb * HG_KEY_DIM, hb * HG_VAL_DIM
    key_spec, val_spec = spec(key_w), spec(val_w)
    grid = (b, HG_HEADS // hb, s // ts)
    w_spec, wb_shape = _cast_rider_specs(w_next, grid)
    return pl.pallas_call(
        kernel,
        grid=grid,
        in_specs=[spec(key_w, col_offsets[0]), key_spec, key_spec, spec(val_w, col_offsets[1]),
                  spec(val_w, col_offsets[2]),
                  pl.BlockSpec((1, HG_VAL_DIM), lambda bi, hi, si: (0, 0)),
                  pl.BlockSpec(e_mat.shape, lambda bi, hi, si: (0, 0)),
                  w_spec],
        out_specs=[val_spec, w_spec],
        out_shape=[jax.ShapeDtypeStruct((b, s, D_INNER), BF16), wb_shape],
        scratch_shapes=[pltpu.VMEM((hb, HG_KEY_DIM, HG_VAL_DIM), F32),
                        pltpu.VMEM((ts // HG_CHUNK,) + (e_mat.shape[0], hb * HG_KEY_DIM), F32),
                        pltpu.VMEM((nhc, HG_CHUNK, HG_KEY_DIM), BF16),
                        pltpu.VMEM((nhc, HG_CHUNK, HG_CHUNK), BF16),
                        pltpu.VMEM((nhc, HG_KEY_DIM, HG_VAL_DIM), F32),
                        pltpu.VMEM((nhc, HG_KEY_DIM, HG_VAL_DIM), BF16)],
        compiler_params=_params(3),
        name="hgrn2_recurrence",
    )(q, lf, k, v, sg, gn, e_mat, w_next)


def _attn_kernel(q_ref, kv_ref, kr_ref, g_ref, w_ref, o_ref, wb_ref, *, tq, heads):
    wb_ref[...] = w_ref[...].astype(BF16)
    s_len = q_ref.shape[1]
    t_i = lax.broadcasted_iota(jnp.int32, (tq, tq), 0)
    s_i = lax.broadcasted_iota(jnp.int32, (tq, tq), 1)
    causal = s_i <= t_i
    ones = jnp.ones((tq, V_DIM), BF16)

    def update(q, keys, mask, m, acc, hh):
        k_cat = jnp.concatenate([kv_ref[0, keys, hh * 256:hh * 256 + NOPE_DIM], kr_ref[0, keys, :]], axis=1)
        v_ext = jnp.concatenate([kv_ref[0, keys, hh * 256 + NOPE_DIM:(hh + 1) * 256], ones], axis=1)
        sc = lax.dot_general(q, k_cat, _NT, preferred_element_type=F32)
        if mask is not None:
            sc = jnp.where(mask, sc, -jnp.inf)
        m_new = jnp.maximum(m, jnp.max(sc, axis=-1, keepdims=True))
        p = jnp.exp2(sc - m_new)
        acc = jnp.exp2(m - m_new) * acc + jnp.dot(p.astype(BF16), v_ext, preferred_element_type=F32)
        return m_new, acc

    for hh in range(heads):
        for qi in range(s_len // tq):
            rows = slice(qi * tq, (qi + 1) * tq)
            q = q_ref[0, rows, hh * 256:(hh + 1) * 256]
            m = jnp.full((tq, 1), -jnp.inf, F32)
            acc = jnp.zeros((tq, 2 * V_DIM), F32)
            for kj in range(qi + 1):
                m, acc = update(q, slice(kj * tq, (kj + 1) * tq), causal if kj == qi else None, m, acc, hh)
            out = acc[:, :V_DIM] / acc[:, V_DIM:] * g_ref[0, rows, hh * V_DIM:(hh + 1) * V_DIM].astype(F32)
            o_ref[0, rows, hh * V_DIM:(hh + 1) * V_DIM] = out.astype(BF16)


def _attn_call(q, kv, kr, gate, w_next, *, tq, heads):
    b, s, _ = q.shape
    kernel = functools.partial(_attn_kernel, tq=tq, heads=heads)
    grid = (b, MLA_HEADS // heads)
    w_spec, wb_shape = _cast_rider_specs(w_next, grid)
    out_spec = pl.BlockSpec((1, s, heads * V_DIM), lambda bi, hp: (bi, 0, hp))
    return pl.pallas_call(
        kernel,
        grid=grid,
        in_specs=[pl.BlockSpec((1, s, heads * 256), lambda bi, hp: (bi, 0, hp)),
                  pl.BlockSpec((1, s, heads * 256), lambda bi, hp: (bi, 0, hp)),
                  pl.BlockSpec((1, s, LANES), lambda bi, hp: (bi, 0, 0)),
                  out_spec, w_spec],
        out_specs=[out_spec, w_spec],
        out_shape=[jax.ShapeDtypeStruct((b, s, D_INNER), BF16), wb_shape],
        compiler_params=_params(2),
        name="mla_flash_attention",
    )(q, kv, kr, gate, w_next)


def _rope_tables(seq):
    pos = np.arange(seq, dtype=np.float64)
    inv_freq = ROPE_THETA ** (-np.arange(0, ROPE_DIM, 2, dtype=np.float64) / ROPE_DIM)
    ang = pos[:, None] * inv_freq[None, :]
    cos, sin = np.cos(ang), np.sin(ang)
    return jnp.asarray(np.concatenate([cos, cos, -sin, sin], axis=-1), dtype=F32)


def _q_pair_selection():
    d_in, d_out = NOPE_DIM + ROPE_DIM, 2 * LANES
    sel = np.zeros((2 * d_in, 2 * d_out), np.float32)
    half = ROPE_DIM // 2
    for h in range(2):
        for c in range(d_in):
            sel[h * d_in + c, h * d_out + c] = 1.0
        for c in range(ROPE_DIM):
            sel[h * d_in + NOPE_DIM + (c + half) % ROPE_DIM, h * d_out + d_in + c] = 1.0
    return sel


def _swap_halves(w):
    half = w.shape[-1] // 2
    return jnp.concatenate([w[..., half:], w[..., :half]], axis=-1)


def kernel(x, norm_g, hg_w_in, hg_g_norm, hg_w_out, hg_lb, kv_in_norm_g, w_kv_down, kv_norm_g,
           w_kv_up, mla_w_in, mla_q_norm_g, mla_w_q_up, mla_w_out, final_norm_g):
    b, s, d = x.shape
    m = b * s
    x2 = x.reshape(m, d)
    tm, tn, sub_n = MM_TM, MM_TN, MM_SUB_N
    rope_table = _rope_tables(s)

    def rope_spec(rows):
        return pl.BlockSpec((rows, LANES), lambda j, i: (i % (s // rows), 0))

    def project(xb, w, n, epilogue, out_dtypes, name, *, col0=0, rows=tm, n_tile=tn, group=sub_n,
                row_group=MM_SUB_M, extras=(), extra_specs=()):
        outs = _mm_call(xb, w, n=n, col0=col0, tm=rows, tn=n_tile, sub_m=row_group, sub_n=group,
                        epilogue=epilogue,
                        extras=extras, extra_specs=extra_specs,
                        out_shapes=[jax.ShapeDtypeStruct((m, n), dt) for dt in out_dtypes],
                        out_specs=[pl.BlockSpec((rows, n_tile), lambda j, i: (i, j)) for _ in out_dtypes],
                        name=name)
        return outs if len(outs) > 1 else outs[0]

    xn = _norm_cast_call(x2, norm_g[0].reshape(1, d), tm=1024, name="hg_norm")
    w_in = hg_w_in[0]
    c1, c2, c3 = HG_KEY_TOTAL, 2 * HG_KEY_TOTAL, 2 * HG_KEY_TOTAL + D_INNER
    lf, kg = project(xn, w_in, c2 - c1, _ep_forget_gate, [F32, BF16], "hg_in_f", col0=c1, n_tile=tn // 2,
                     row_group=tm // 2, extras=(hg_lb,),
                     extra_specs=(pl.BlockSpec((hg_lb.shape[0], tn // 2), lambda j, i: (0, j)),))
    f_tiles = (c2 - c1) // tn
    n_qig = w_in.shape[1] - (c2 - c1)
    qig = _mm_call(xn, w_in, n=n_qig, tm=tm, tn=tn, sub_m=MM_SUB_M, sub_n=sub_n,
                   epilogue=((0, _ep_cast), ((c3 - (c2 - c1)) // tn, _ep_silu)),
                   w_tile_of=lambda j: jnp.where(j < c1 // tn, j, j + f_tiles),
                   out_shapes=[jax.ShapeDtypeStruct((m, n_qig), BF16)],
                   out_specs=[pl.BlockSpec((tm, tn), lambda j, i: (i, j))], name="hg_in_qig")[0]
    qig = qig.reshape(b, s, -1)

    y, w_out0 = _hgrn_call(qig, lf.reshape(b, s, -1), kg.reshape(b, s, -1), qig, qig,
                           hg_g_norm[0].reshape(1, HG_VAL_DIM), hg_w_out[0], hb=2, ts=1024,
                           col_offsets=(0, c1, c1 + (c3 - c2)))
    h1, h1n_kv, h1n_q = _mm_res_call(y.reshape(m, D_INNER), w_out0, x2,
                                     (kv_in_norm_g.reshape(1, d), norm_g[1].reshape(1, d)),
                                     final_norm=False, tm=RES_TM, sub_n=RES_SUB_N, name="hg_out")

    w_kvd = jnp.concatenate([w_kv_down, _swap_halves(w_kv_down[:, KV_LORA:])], axis=-1)
    n_kvd = w_kvd.shape[1]
    cn, kr = _mm_call(h1n_kv, w_kvd, n=n_kvd, tm=1024, tn=n_kvd, sub_m=MM_SUB_M, sub_n=n_kvd,
                      epilogue=_ep_kv_down,
                      extras=(kv_norm_g.reshape(1, KV_LORA), rope_table),
                      extra_specs=(pl.BlockSpec((1, KV_LORA), lambda j, i: (0, 0)), rope_spec(1024)),
                      out_shapes=[jax.ShapeDtypeStruct((m, KV_LORA), BF16),
                                  jax.ShapeDtypeStruct((m, LANES), BF16)],
                      out_specs=[pl.BlockSpec((1024, KV_LORA), lambda j, i: (i, 0)),
                                 pl.BlockSpec((1024, LANES), lambda j, i: (i, 0))], name="kv_down")
    kv = project(cn, w_kv_up, w_kv_up.shape[1], _ep_cast, [BF16], "kv_up", n_tile=2 * tn)

    w_in1 = mla_w_in[0]
    cqn = _mm_call(h1n_q, w_in1, n=Q_LORA, tm=1024, tn=Q_LORA, sub_m=MM_SUB_M, sub_n=Q_LORA,
                   epilogue=_ep_rms,
                   extras=(mla_q_norm_g[0].reshape(1, Q_LORA),),
                   extra_specs=(pl.BlockSpec((1, Q_LORA), lambda j, i: (0, 0)),),
                   out_shapes=[jax.ShapeDtypeStruct((m, Q_LORA), BF16)],
                   out_specs=[pl.BlockSpec((1024, Q_LORA), lambda j, i: (i, 0))], name="mla_in_q")[0]
    gate = project(h1n_q, w_in1, w_in1.shape[1] - Q_LORA, _ep_silu, [BF16], "mla_in_gate", col0=Q_LORA)

    sel = jnp.asarray(_q_pair_selection(), dtype=BF16)
    heads_per_tile = 2 * tn // (2 * LANES)
    n_q = MLA_HEADS * 2 * LANES
    qf = _mm_call(cqn, mla_w_q_up[0], n=n_q, tm=tm, tn=2 * tn, w_tn=heads_per_tile * (NOPE_DIM + ROPE_DIM),
                  sub_m=MM_SUB_M, sub_n=512, epilogue=_ep_q_up, prep=_prep_q_up,
                  extras=(rope_table, sel),
                  extra_specs=(rope_spec(tm), pl.BlockSpec(sel.shape, lambda j, i: (0, 0))),
                  out_shapes=[jax.ShapeDtypeStruct((m, n_q), BF16)],
                  out_specs=[pl.BlockSpec((tm, 2 * tn), lambda j, i: (i, j))], name="mla_q_up")[0]

    attn, w_out1 = _attn_call(qf.reshape(b, s, -1), kv.reshape(b, s, -1), kr.reshape(b, s, -1),
                              gate.reshape(b, s, -1), mla_w_out[0], tq=512, heads=4)
    out = _mm_res_call(attn.reshape(m, D_INNER), w_out1, h1,
                       (final_norm_g.reshape(1, d),), final_norm=True, tm=RES_TM, sub_n=RES_SUB_N,
                       name="mla_out")[0]
    return out.reshape(b, s, d)
```
